```python
import jax, jax.numpy as jnp
from jax import lax
import numpy as np

D_MODEL = 1024
BATCH = 8
SEQ = 4096
DEPTH = 2

CTX_LEN = 256
GRID_W = 64

GLA_HEADS = 4
GLA_DK = D_MODEL // 2 // GLA_HEADS
GLA_DV = D_MODEL // GLA_HEADS
GLA_K = GLA_HEADS * GLA_DK
GLA_V = GLA_HEADS * GLA_DV
GLA_LR = 16
GLA_TAU = 16.0
GLA_CHUNK = 64

CONV_W = D_MODEL // 2
CONV_K = 31

POOL_GROUPS = 4
POOL_W = D_MODEL // 2
POOL_GC = POOL_W // POOL_GROUPS
POOL_WINDOWS = (2, 4, 8, 16)

N_BRANCH = 3
D_FF = 4 * D_MODEL
EPS = 1e-6

SPLIT_SIZES = (GLA_K, GLA_K, GLA_V, GLA_V, GLA_LR, GLA_LR, CONV_W, CONV_W, POOL_W, N_BRANCH * D_MODEL)
IN_COLS = 2 * GLA_K + 2 * GLA_V + 2 * GLA_LR + 2 * CONV_W + POOL_W + N_BRANCH * D_MODEL

kernel_name = "hybrid_gla_conformer_pool_dit_block"


def rms_norm(x, g):
    x32 = x.astype(jnp.float32)
    y = x32 * lax.rsqrt(jnp.mean(x32 * x32, axis=-1, keepdims=True) + EPS)
    return (y * g.astype(jnp.float32)).astype(x.dtype)


def layer_norm(x, g, b):
    x32 = x.astype(jnp.float32)
    mu = jnp.mean(x32, axis=-1, keepdims=True)
    xc = x32 - mu
    y = xc * lax.rsqrt(jnp.mean(xc * xc, axis=-1, keepdims=True) + EPS)
    return (y * g.astype(jnp.float32) + b.astype(jnp.float32)).astype(x.dtype)


def modulate(x, shift, scale):
    return x * (1.0 + scale) + shift


def in_projection(h, w_in):
    parts, start = [], 0
    for size in SPLIT_SIZES:
        parts.append(h @ w_in[:, start:start + size])
        start += size
    return parts


def gla_chunk(q, k, v, la, s0):
    B, L, H, DK = q.shape
    DV = v.shape[-1]
    C = GLA_CHUNK
    nC = L // C
    q, k, la = (t.reshape(B, nC, C, H, DK) for t in (q, k, la))
    v = v.reshape(B, nC, C, H, DV)
    b = jnp.cumsum(la, axis=2)
    q_i = q * jnp.exp(b)
    k_i = k * jnp.exp(-b)
    mask = jnp.tril(jnp.ones((C, C), dtype=bool))
    att = jnp.einsum('bnthd,bnshd->bnhts', q_i, k_i)
    att = jnp.where(mask, att, 0.0)
    o = jnp.einsum('bnhts,bnshv->bnthv', att, v)
    b_end = b[:, :, -1]
    k_end = k * jnp.exp(b_end[:, :, None] - b)
    d_state = jnp.einsum('bnshd,bnshv->bnhdv', k_end, v)
    gamma = jnp.exp(b_end)

    def step(s, inp):
        g, ds = inp
        return g[..., None] * s + ds, s

    s_fin, s_start = lax.scan(step, s0, (jnp.moveaxis(gamma, 1, 0), jnp.moveaxis(d_state, 1, 0)))
    s_start = jnp.moveaxis(s_start, 0, 1)
    o = o + jnp.einsum('bnthd,bnhdv->bnthv', q_i, s_start)
    return o.reshape(B, L, H, DV), s_fin


def gla_bidir(q, k, v, la_f, la_b, s_f0, s_b0):
    o_f, s_f = gla_chunk(q, k, v, la_f, s_f0)
    rev = lambda t: jnp.flip(t, axis=1)
    o_b, s_b = gla_chunk(rev(q), rev(k), rev(v), rev(la_b), s_b0)
    return o_f + rev(o_b), s_f, s_b


def gla_inputs(parts, p):
    pq, pk, pv, _, plf, plb = parts[:6]
    B, L, _ = pq.shape
    q = pq.astype(jnp.float32).reshape(B, L, GLA_HEADS, GLA_DK) * (GLA_DK ** -0.5)
    k = pk.astype(jnp.float32).reshape(B, L, GLA_HEADS, GLA_DK)
    v = pv.astype(jnp.float32).reshape(B, L, GLA_HEADS, GLA_DV)

    def log_decay(plr, i):
        z = (plr @ p['w_decay'][i] + p['b_decay'][i]).astype(jnp.float32)
        return (jax.nn.log_sigmoid(z) / GLA_TAU).reshape(B, L, GLA_HEADS, GLA_DK)

    return q, k, v, log_decay(plf, 0), log_decay(plb, 1)


def gla_out(o, pg, p):
    B, L = o.shape[:2]
    o = o * lax.rsqrt(jnp.mean(o * o, axis=-1, keepdims=True) + EPS)
    o = o * p['g_gla'].astype(jnp.float32).reshape(GLA_HEADS, GLA_DV)
    o = o.reshape(B, L, GLA_V).astype(pg.dtype) * jax.nn.silu(pg)
    return o @ p['w_gla_o']


def depthwise_conv(u, w, b):
    C = u.shape[-1]
    y = lax.conv_general_dilated(u, w[:, None, :].astype(u.dtype), window_strides=(1,),
                                 padding=[(CONV_K // 2, CONV_K // 2)],
                                 dimension_numbers=('NWC', 'WIO', 'NWC'),
                                 feature_group_count=C)
    return y + b


def conv_branch(pa, pb, p, rows):
    u = pa * jax.nn.sigmoid(pb)
    B, L, C = u.shape
    seqs = u if rows is None else u.reshape(B * rows, GRID_W, C)
    y = depthwise_conv(seqs, p['w_dw'], p['b_dw']).reshape(B, L, C)
    y = jax.nn.silu(layer_norm(y, p['g_conv_ln'], p['b_conv_ln']))
    return y @ p['w_conv_o']


def window_mean(u, w):
    L = u.shape[1]
    left = w // 2
    right = w - 1 - left
    cs = jnp.cumsum(u.astype(jnp.float32), axis=1)
    cs = jnp.concatenate([jnp.zeros_like(cs[:, :1]), cs], axis=1)
    t = jnp.arange(L)
    lo = jnp.clip(t - left, 0, L)
    hi = jnp.clip(t + right + 1, 0, L)
    total = jnp.take(cs, hi, axis=1) - jnp.take(cs, lo, axis=1)
    cnt = (hi - lo).astype(jnp.float32).reshape((1, L) + (1,) * (u.ndim - 2))
    return (total / cnt).astype(u.dtype)


def pool_branch(u, p, rows):
    B, L, C = u.shape
    if rows is None:
        grid = u.reshape(B, L, POOL_GROUPS, POOL_GC)
    else:
        grid = u.reshape(B, rows, GRID_W, POOL_GROUPS, POOL_GC)
    pooled = jnp.stack([window_mean(grid[..., i, :], w) for i, w in enumerate(POOL_WINDOWS)], axis=-2)
    y = jnp.einsum('...gc,gcd->...gd', pooled - grid, p['w_pool_g'])
    y = y.reshape(B, L, C) * p['s_pool']
    return y @ p['w_pool_o']


def merge(ya, yb, yc, pgate, p):
    B, L, _ = pgate.shape
    gates = jax.nn.sigmoid(pgate.reshape(B, L, N_BRANCH, D_MODEL) + p['b_gate'])
    mixed = gates[:, :, 0] * ya + gates[:, :, 1] * yb + gates[:, :, 2] * yc
    return mixed @ p['w_out']


def token_mixer(h, hc, p, need_ctx_out):
    rows = h.shape[1] // GRID_W
    parts = in_projection(h, p['w_in'])
    cparts = in_projection(hc, p['w_in'])
    zero = jnp.zeros((hc.shape[0], GLA_HEADS, GLA_DK, GLA_DV), jnp.float32)
    cq, ck, cv, cla_f, cla_b = gla_inputs(cparts, p)
    co, cs_f, cs_b = gla_bidir(cq, ck, cv, cla_f, cla_b, zero, zero)
    q, k, v, la_f, la_b = gla_inputs(parts, p)
    o, _, _ = gla_bidir(q, k, v, la_f, la_b, cs_f, cs_b)
    ya = gla_out(o, parts[3], p)
    yb = conv_branch(parts[6], parts[7], p, rows)
    yc = pool_branch(parts[8], p, rows)
    y = merge(ya, yb, yc, parts[9], p)
    if not need_ctx_out:
        return y, None
    ca = gla_out(co, cparts[3], p)
    cb = conv_branch(cparts[6], cparts[7], p, None)
    cc = pool_branch(cparts[8], p, None)
    y_ctx = merge(ca, cb, cc, cparts[9], p)
    return y, y_ctx


def sq_relu_mlp(h, w1, w2):
    return jnp.square(jax.nn.relu(h @ w1)) @ w2


def setup_inputs(seed: int = 0) -> dict:
    key = jax.random.key(seed)
    ks = iter(jax.random.split(key, 32))

    def nrm(shape, scale):
        return jax.random.normal(next(ks), shape, jnp.float32) * scale

    L = DEPTH
    return {
        'x': nrm((BATCH, SEQ, D_MODEL), 1.0),
        'c': nrm((BATCH, D_MODEL), 1.0),
        'ctx': nrm((BATCH, CTX_LEN, D_MODEL), 1.0),
        'c_ctx': nrm((D_MODEL,), 1.0),
        'w_ada': nrm((L, D_MODEL, 6 * D_MODEL), D_MODEL ** -0.5),
        'b_ada': nrm((L, 6 * D_MODEL), 0.02),
        'g_pre_mix': 1.0 + nrm((L, D_MODEL), 0.05),
        'g_post_mix': 1.0 + nrm((L, D_MODEL), 0.05),
        'g_pre_mlp': 1.0 + nrm((L, D_MODEL), 0.05),
        'g_post_mlp': 1.0 + nrm((L, D_MODEL), 0.05),
        'w_in': nrm((L, D_MODEL, IN_COLS), D_MODEL ** -0.5),
        'w_decay': nrm((L, 2, GLA_LR, GLA_K), GLA_LR ** -0.5),
        'b_decay': nrm((L, 2, GLA_K), 0.1),
        'g_gla': 1.0 + nrm((L, GLA_V), 0.05),
        'w_gla_o': nrm((L, GLA_V, D_MODEL), GLA_V ** -0.5),
        'w_dw': nrm((L, CONV_K, CONV_W), CONV_K ** -0.5),
        'b_dw': nrm((L, CONV_W), 0.02),
        'g_conv_ln': 1.0 + nrm((L, CONV_W), 0.05),
        'b_conv_ln': nrm((L, CONV_W), 0.02),
        'w_conv_o': nrm((L, CONV_W, D_MODEL), CONV_W ** -0.5),
        'w_pool_g': nrm((L, POOL_GROUPS, POOL_GC, POOL_GC), POOL_GC ** -0.5),
        's_pool': 1.0 + nrm((L, POOL_W), 0.1),
        'w_pool_o': nrm((L, POOL_W, D_MODEL), POOL_W ** -0.5),
        'b_gate': nrm((L, N_BRANCH, D_MODEL), 0.1),
        'w_out': nrm((L, D_MODEL, D_MODEL), D_MODEL ** -0.5),
        'w_mlp1': nrm((L, D_MODEL, D_FF), D_MODEL ** -0.5),
        'w_mlp2': nrm((L, D_FF, D_MODEL), D_FF ** -0.5),
    }


def reference(x, c, ctx, c_ctx, w_ada, b_ada, g_pre_mix, g_post_mix, g_pre_mlp, g_post_mlp,
              w_in, w_decay, b_decay, g_gla, w_gla_o, w_dw, b_dw, g_conv_ln, b_conv_ln, w_conv_o,
              w_pool_g, s_pool, w_pool_o, b_gate, w_out, w_mlp1, w_mlp2):
    silu_c = jax.nn.silu(c)
    silu_cc = jax.nn.silu(c_ctx)
    for l in range(DEPTH):
        last = l == DEPTH - 1
        p = {
            'w_in': w_in[l], 'w_decay': w_decay[l], 'b_decay': b_decay[l], 'g_gla': g_gla[l],
            'w_gla_o': w_gla_o[l], 'w_dw': w_dw[l], 'b_dw': b_dw[l], 'g_conv_ln': g_conv_ln[l],
            'b_conv_ln': b_conv_ln[l], 'w_conv_o': w_conv_o[l], 'w_pool_g': w_pool_g[l],
            's_pool': s_pool[l], 'w_pool_o': w_pool_o[l], 'b_gate': b_gate[l], 'w_out': w_out[l],
        }
        mod = jnp.split((silu_c @ w_ada[l] + b_ada[l])[:, None, :], 6, axis=-1)
        mod_c = jnp.split(silu_cc @ w_ada[l] + b_ada[l], 6, axis=-1)

        h = modulate(rms_norm(x, g_pre_mix[l]), mod[0], mod[1])
        hc = modulate(rms_norm(ctx, g_pre_mix[l]), mod_c[0], mod_c[1])
        y, y_ctx = token_mixer(h, hc, p, not last)
        x = x + mod[2] * rms_norm(y, g_post_mix[l])
        h = modulate(rms_norm(x, g_pre_mlp[l]), mod[3], mod[4])
        x = x + mod[5] * rms_norm(sq_relu_mlp(h, w_mlp1[l], w_mlp2[l]), g_post_mlp[l])

        if not last:
            ctx = ctx + mod_c[2] * rms_norm(y_ctx, g_post_mix[l])
            hc = modulate(rms_norm(ctx, g_pre_mlp[l]), mod_c[3], mod_c[4])
            ctx = ctx + mod_c[5] * rms_norm(sq_relu_mlp(hc, w_mlp1[l], w_mlp2[l]), g_post_mlp[l])
    return x
```

```python
import functools

import jax
import jax.numpy as jnp
from jax import lax
from jax.experimental import pallas as pl
from jax.experimental.pallas import tpu as pltpu

D_MODEL = 1024
DEPTH = 2
GRID_W = 64
GLA_HEADS = 4
GLA_DK = 128
GLA_DV = 256
GLA_K = GLA_HEADS * GLA_DK
GLA_V = GLA_HEADS * GLA_DV
GLA_LR = 16
GLA_TAU = 16.0
GLA_CHUNK = 64
CONV_W = 512
CONV_K = 31
CONV_HALO = 16
POOL_GROUPS = 4
POOL_W = 512
POOL_GC = 128
POOL_WINDOWS = (2, 4, 8, 16)
N_BRANCH = 3
D_FF = 4 * D_MODEL
EPS = 1e-6
LANES = 128
LR_PAD = LANES
MOD_ROWS = 16
VMEM_LIMIT = 56 * 1024 * 1024

F32 = jnp.float32
BF16 = jnp.bfloat16


def _dot(a, b):
    return jnp.dot(a, b, preferred_element_type=F32)


def _dot_nt(a, b):
    return lax.dot_general(a, b, (((1,), (1,)), ((), ())), preferred_element_type=F32)


def _dot_tn(a, b):
    return lax.dot_general(a, b, (((0,), (0,)), ((), ())), preferred_element_type=F32)


def _sigmoid(x):
    return jax.nn.sigmoid(x)


def _silu(x):
    return x * _sigmoid(x)


def _rms(x):
    return x * lax.rsqrt(jnp.mean(x * x, axis=-1, keepdims=True) + EPS)


def _cparams(sem):
    return pltpu.CompilerParams(dimension_semantics=sem, vmem_limit_bytes=VMEM_LIMIT)


def _full(shape):
    nd = len(shape)
    return pl.BlockSpec(shape, lambda *_: (0,) * nd)


def _ada_kernel(cc_ref, w_ref, b_ref, o_ref):
    a = _silu(cc_ref[...]).astype(BF16)
    o_ref[0] = _dot(a, w_ref[0].astype(BF16)) + b_ref[0]


def _ada(cc, w_ada, b_ada):
    tn = 512
    n = 6 * D_MODEL
    return pl.pallas_call(
        _ada_kernel,
        grid=(DEPTH, n // tn),
        in_specs=[
            pl.BlockSpec((MOD_ROWS, D_MODEL), lambda l, j: (0, 0)),
            pl.BlockSpec((1, D_MODEL, tn), lambda l, j: (l, 0, j)),
            pl.BlockSpec((1, 1, tn), lambda l, j: (l, 0, j)),
        ],
        out_specs=pl.BlockSpec((1, MOD_ROWS, tn), lambda l, j: (l, 0, j)),
        out_shape=jax.ShapeDtypeStruct((DEPTH, MOD_ROWS, n), F32),
        compiler_params=_cparams(("parallel", "parallel")),
        name="ada",
    )(cc, w_ada, b_ada.reshape(DEPTH, 1, n))


def _pre_kernel(x_ref, mod_ref, g_ref, wq_ref, wk_ref, wv_ref, wg_ref, wlr_ref, wga_ref, wgb_ref, wp_ref,
                wdw_ref, bdw_ref, gln_ref, bln_ref,
                q_ref, k_ref, v_ref, sg_ref, lr_ref, cv_ref, up_ref, u_scr, y_scr, *, row_len):
    tm = x_ref.shape[0]
    h = _rms(x_ref[...]) * g_ref[...]
    h = h * (1.0 + mod_ref[0, 1:2, :]) + mod_ref[0, 0:1, :]
    hb = h.astype(BF16)
    q_ref[...] = (_dot(hb, wq_ref[...]) * (GLA_DK ** -0.5)).astype(BF16)
    k_ref[...] = _dot(hb, wk_ref[...]).astype(BF16)
    v_ref[...] = _dot(hb, wv_ref[...]).astype(BF16)
    sg_ref[...] = _silu(_dot(hb, wg_ref[...])).astype(BF16)
    lr_ref[...] = _dot(hb, wlr_ref[...]).astype(BF16)
    up_ref[...] = _dot(hb, wp_ref[...]).astype(BF16)
    u_scr[...] = _dot(hb, wga_ref[...]) * _sigmoid(_dot(hb, wgb_ref[...]))

    wp = row_len + 2 * CONV_HALO
    zeros = jnp.zeros((CONV_HALO, LANES), F32)

    def strip(r, carry):
        base = pl.multiple_of(r * row_len, row_len)
        for c in range(CONV_W // LANES):
            cs = slice(c * LANES, (c + 1) * LANES)
            s = jnp.concatenate([zeros, u_scr[pl.ds(base, row_len), cs], zeros], axis=0)
            acc = jnp.zeros((row_len, LANES), F32)
            for j in range(CONV_K):
                d = j - CONV_K // 2
                sh = s if d == 0 else pltpu.roll(s, (wp - d) % wp, axis=0)
                acc = acc + sh[CONV_HALO:CONV_HALO + row_len, :] * wdw_ref[j:j + 1, cs]
            y_scr[pl.ds(base, row_len), cs] = acc + bdw_ref[:, cs]
        return carry

    lax.fori_loop(0, tm // row_len, strip, 0)

    y = y_scr[...]
    yc = y - jnp.mean(y, axis=-1, keepdims=True)
    yn = yc * lax.rsqrt(jnp.mean(yc * yc, axis=-1, keepdims=True) + EPS) * gln_ref[...] + bln_ref[...]
    cv_ref[...] = _silu(yn).astype(BF16)


def _pre(xf, mod, g, w, conv, *, tm, row_len, mod_index):
    n = xf.shape[0]
    tok = lambda c: pl.BlockSpec((tm, c), lambda i: (i, 0))
    out_cols = (GLA_K, GLA_K, GLA_V, GLA_V, LR_PAD, CONV_W, POOL_W)
    return pl.pallas_call(
        functools.partial(_pre_kernel, row_len=row_len),
        grid=(n // tm,),
        in_specs=[tok(D_MODEL),
                  pl.BlockSpec((1, 6, D_MODEL), lambda i: (mod_index(i), 0, 0)),
                  _full((1, D_MODEL))]
                 + [_full(a.shape) for a in w]
                 + [_full(a.shape) for a in conv],
        out_specs=[tok(c) for c in out_cols],
        out_shape=[jax.ShapeDtypeStruct((n, c), BF16) for c in out_cols],
        scratch_shapes=[pltpu.VMEM((tm, CONV_W), F32), pltpu.VMEM((tm, CONV_W), F32)],
        compiler_params=_cparams(("parallel",)),
        name="pre",
    )(xf, mod, g, *w, *conv)


def _gla_kernel(qf_ref, kf_ref, vf_ref, lrf_ref, qb_ref, kb_ref, vb_ref, lrb_ref, wd_ref, bd_ref,
                s0f_ref, s0b_ref, of_ref, ob_ref, sf_ref, sb_ref):
    t = qf_ref.shape[0]
    nc = t // GLA_CHUNK

    @pl.when(pl.program_id(1) == 0)
    def _():
        sf_ref[...] = s0f_ref[...]
        sb_ref[...] = s0b_ref[...]

    row = lax.broadcasted_iota(jnp.int32, (t, t), 0)
    col = lax.broadcasted_iota(jnp.int32, (t, t), 1)
    shift = GLA_CHUNK.bit_length() - 1
    same = jnp.right_shift(row, shift) == jnp.right_shift(col, shift)
    crow = lax.broadcasted_iota(jnp.int32, (GLA_CHUNK, GLA_CHUNK), 0)
    ccol = lax.broadcasted_iota(jnp.int32, (GLA_CHUNK, GLA_CHUNK), 1)

    def direction(q_ref, k_ref, v_ref, lr_ref, o_ref, s_ref, d):
        fwd = d == 0
        ks = slice(d * GLA_K, (d + 1) * GLA_K)
        z = _dot(lr_ref[...], wd_ref[:, ks]) + bd_ref[:, ks]
        la = (jnp.minimum(z, 0.0) - jnp.log1p(jnp.exp(-jnp.abs(z)))) * (1.0 / GLA_TAU)
        tri = jnp.where(same & ((col <= row) if fwd else (col >= row)), 1.0, 0.0).astype(BF16)
        hi = la.astype(BF16)
        lo = (la - hi.astype(F32)).astype(BF16)
        b = _dot(tri, hi) + _dot(tri, lo)
        qi = (q_ref[...].astype(F32) * jnp.exp(b)).astype(BF16)
        ki = k_ref[...].astype(F32) * jnp.exp(-b)
        kib = ki.astype(BF16)
        amask = (ccol <= crow) if fwd else (ccol >= crow)
        for c in (range(nc) if fwd else reversed(range(nc))):
            rows = slice(c * GLA_CHUNK, (c + 1) * GLA_CHUNK)
            e = (c + 1) * GLA_CHUNK - 1 if fwd else c * GLA_CHUNK
            gamma = jnp.exp(b[e:e + 1, :])
            kend = (ki[rows, :] * gamma).astype(BF16)
            for h in range(GLA_HEADS):
                kc = slice(h * GLA_DK, (h + 1) * GLA_DK)
                vc = slice(h * GLA_DV, (h + 1) * GLA_DV)
                qh = qi[rows, kc]
                att = jnp.where(amask, _dot_nt(qh, kib[rows, kc]), 0.0).astype(BF16)
                vh = v_ref[rows, vc]
                st = s_ref[0, h]
                o = _dot(att, vh) + _dot_nt(qh, st.astype(BF16))
                o_ref[rows, vc] = o.astype(o_ref.dtype)
                s_ref[0, h] = st * gamma[:, kc] + _dot_tn(vh, kend[:, kc])

    direction(qf_ref, kf_ref, vf_ref, lrf_ref, of_ref, sf_ref, 0)
    direction(qb_ref, kb_ref, vb_ref, lrb_ref, ob_ref, sb_ref, 1)


def _gla(q, k, v, lr, wd, bd, s0f, s0b, *, batch, t):
    n = q.shape[0]
    nt = n // batch // t
    fwd = lambda c: pl.BlockSpec((t, c), lambda b, i: (b * nt + i, 0))
    bwd = lambda c: pl.BlockSpec((t, c), lambda b, i: (b * nt + nt - 1 - i, 0))
    st = pl.BlockSpec((1, GLA_HEADS, GLA_DV, GLA_DK), lambda b, i: (b, 0, 0, 0))
    st_shape = jax.ShapeDtypeStruct((batch, GLA_HEADS, GLA_DV, GLA_DK), F32)
    return pl.pallas_call(
        _gla_kernel,
        grid=(batch, nt),
        in_specs=[fwd(GLA_K), fwd(GLA_K), fwd(GLA_V), fwd(LR_PAD),
                  bwd(GLA_K), bwd(GLA_K), bwd(GLA_V), bwd(LR_PAD),
                  _full(wd.shape), _full(bd.shape), st, st],
        out_specs=[fwd(GLA_V), bwd(GLA_V), st, st],
        out_shape=[jax.ShapeDtypeStruct((n, GLA_V), BF16), jax.ShapeDtypeStruct((n, GLA_V), BF16),
                   st_shape, st_shape],
        compiler_params=_cparams(("parallel", "arbitrary")),
        name="gla",
    )(q, k, v, lr, q, k, v, lr, wd, bd, s0f, s0b)


def _window_counts(n, left, right, shape, axis):
    t = lax.broadcasted_iota(jnp.int32, shape, axis)
    lo = jnp.clip(t - left, 0, n)
    hi = jnp.clip(t + right + 1, 0, n)
    return (hi - lo).astype(F32)


def _mix_kernel(x_ref, mod_ref, gpre_ref, gpost_ref, of_ref, ob_ref, sg_ref, cv_ref, up_ref,
                wgt_ref, bgt_ref, ggla_ref, wgo_ref, wco_ref, wpg_ref, sp_ref, wpo_ref, wout_ref,
                o_ref, pd_scr, *, lead):
    tm = pd_scr.shape[0] if not lead else x_ref.shape[0] * x_ref.shape[1]
    ld = lambda ref: ref[...].reshape(tm, ref.shape[-1])

    @pl.when(pl.program_id(2) == 0)
    def _():
        up = up_ref[...].astype(F32)
        n = up.shape[0]
        for g, w in enumerate(POOL_WINDOWS):
            left, right = w // 2, w - 1 - w // 2
            ug = up[..., g * POOL_GC:(g + 1) * POOL_GC]
            total = ug
            for d in range(-left, right + 1):
                if d == 0:
                    continue
                if lead:
                    pad = jnp.zeros((abs(d),) + ug.shape[1:], F32)
                    sh = (jnp.concatenate([ug[d:], pad], axis=0) if d > 0
                          else jnp.concatenate([pad, ug[:n + d]], axis=0))
                else:
                    tpos = lax.broadcasted_iota(jnp.int32, ug.shape, 0)
                    sh = jnp.where((tpos + d >= 0) & (tpos + d < n),
                                   pltpu.roll(ug, (n - d) % n, axis=0), 0.0)
                total = total + sh
            cnt = _window_counts(n, left, right, (n,) + (1,) * (ug.ndim - 1), 0)
            diff = (total / cnt - ug).reshape(-1, POOL_GC).astype(BF16)
            cs = slice(g * POOL_GC, (g + 1) * POOL_GC)
            pd_scr[:, cs] = (_dot(diff, wpg_ref[g]) * sp_ref[:, cs]).astype(BF16)

    x = ld(x_ref)
    h = _rms(x) * gpre_ref[...]
    hb = (h * (1.0 + mod_ref[0, 1:2, :]) + mod_ref[0, 0:1, :]).astype(BF16)

    o = ld(of_ref).astype(F32) + ld(ob_ref).astype(F32)
    on = jnp.concatenate([_rms(o[:, hh * GLA_DV:(hh + 1) * GLA_DV]) for hh in range(GLA_HEADS)], axis=-1)
    a_in = (on * ggla_ref[...] * ld(sg_ref).astype(F32)).astype(BF16)
    ya = _dot(a_in, wgo_ref[...])
    gate = lambda i: _sigmoid(_dot(hb, wgt_ref[:, i * D_MODEL:(i + 1) * D_MODEL])
                              + bgt_ref[:, i * D_MODEL:(i + 1) * D_MODEL])
    mixed = gate(0) * ya
    mixed = mixed + gate(1) * _dot(ld(cv_ref), wco_ref[...])
    if lead:
        base = pl.multiple_of(pl.program_id(2) * tm, tm)
        pd = pd_scr[pl.ds(base, tm), :]
    else:
        pd = pd_scr[...]
    mixed = mixed + gate(2) * _dot(pd, wpo_ref[...])
    y = _dot(mixed.astype(BF16), wout_ref[...])
    out = x + mod_ref[0, 2:3, :] * (_rms(y) * gpost_ref[...])
    o_ref[...] = out.reshape(o_ref.shape)


def _mix_weight_specs(ws):
    return [_full(a.shape) for a in ws]


def _mix_latent(x4, mod, gpre, gpost, o_f, o_b, sg, cv, up, ws, *, rb, cb):
    bsz, rows, cols, _ = x4.shape
    blk = lambda c: pl.BlockSpec((None, rb, cb, c), lambda b, ci, ri: (b, ri, ci, 0))
    return pl.pallas_call(
        functools.partial(_mix_kernel, lead=True),
        grid=(bsz, cols // cb, rows // rb),
        in_specs=[blk(D_MODEL),
                  pl.BlockSpec((1, 6, D_MODEL), lambda b, ci, ri: (b, 0, 0)),
                  _full((1, D_MODEL)), _full((1, D_MODEL)),
                  blk(GLA_V), blk(GLA_V), blk(GLA_V), blk(CONV_W),
                  pl.BlockSpec((None, rows, cb, POOL_W), lambda b, ci, ri: (b, 0, ci, 0))]
                 + _mix_weight_specs(ws),
        out_specs=blk(D_MODEL),
        out_shape=jax.ShapeDtypeStruct(x4.shape, F32),
        scratch_shapes=[pltpu.VMEM((rows * cb, POOL_W), BF16)],
        compiler_params=_cparams(("parallel", "parallel", "arbitrary")),
        name="mix",
    )(x4, mod, gpre, gpost, o_f, o_b, sg, cv, up, *ws)


def _mix_ctx(xf, mod, gpre, gpost, o_f, o_b, sg, cv, up, ws, *, batch):
    n = xf.shape[0]
    tm = n // batch
    blk = lambda c: pl.BlockSpec((tm, c), lambda b, ci, ri: (b, 0))
    return pl.pallas_call(
        functools.partial(_mix_kernel, lead=False),
        grid=(batch, 1, 1),
        in_specs=[blk(D_MODEL),
                  pl.BlockSpec((1, 6, D_MODEL), lambda b, ci, ri: (batch, 0, 0)),
                  _full((1, D_MODEL)), _full((1, D_MODEL)),
                  blk(GLA_V), blk(GLA_V), blk(GLA_V), blk(CONV_W), blk(POOL_W)]
                 + _mix_weight_specs(ws),
        out_specs=blk(D_MODEL),
        out_shape=jax.ShapeDtypeStruct(xf.shape, F32),
        scratch_shapes=[pltpu.VMEM((tm, POOL_W), BF16)],
        compiler_params=_cparams(("parallel", "parallel", "arbitrary")),
        name="mix_ctx",
    )(xf, mod, gpre, gpost, o_f, o_b, sg, cv, up, *ws)


def _mlp_kernel(x_ref, mod_ref, gpre_ref, gpost_ref, w1_ref, w2_ref, o_ref):
    x = x_ref[...]
    h = _rms(x) * gpre_ref[...]
    hb = (h * (1.0 + mod_ref[0, 4:5, :]) + mod_ref[0, 3:4, :]).astype(BF16)
    y = jnp.zeros(x.shape, F32)
    for c in range(D_FF // D_MODEL):
        cs = slice(c * D_MODEL, (c + 1) * D_MODEL)
        a = jnp.maximum(_dot(hb, w1_ref[:, cs]), 0.0)
        y = y + _dot((a * a).astype(BF16), w2_ref[cs, :])
    o_ref[...] = x + mod_ref[0, 5:6, :] * (_rms(y) * gpost_ref[...])


def _mlp(xf, mod, gpre, gpost, w1, w2, *, tm, mod_index):
    n = xf.shape[0]
    tok = pl.BlockSpec((tm, D_MODEL), lambda i: (i, 0))
    return pl.pallas_call(
        _mlp_kernel,
        grid=(n // tm,),
        in_specs=[tok, pl.BlockSpec((1, 6, D_MODEL), lambda i: (mod_index(i), 0, 0)),
                  _full((1, D_MODEL)), _full((1, D_MODEL)), _full(w1.shape), _full(w2.shape)],
        out_specs=tok,
        out_shape=jax.ShapeDtypeStruct(xf.shape, F32),
        compiler_params=_cparams(("parallel",)),
        name="mlp",
    )(xf, mod, gpre, gpost, w1, w2)


def kernel(x, c, ctx, c_ctx, w_ada, b_ada, g_pre_mix, g_post_mix, g_pre_mlp, g_post_mlp, w_in, w_decay,
           b_decay, g_gla, w_gla_o, w_dw, b_dw, g_conv_ln, b_conv_ln, w_conv_o, w_pool_g, s_pool,
           w_pool_o, b_gate, w_out, w_mlp1, w_mlp2):
    bsz, seq, d = x.shape
    ctx_len = ctx.shape[1]
    rows = seq // GRID_W
    n_lat, n_ctx = bsz * seq, bsz * ctx_len
    tm = 512
    row = lambda a: a.reshape(1, -1)

    cc = jnp.zeros((MOD_ROWS, d), F32).at[:bsz].set(c).at[bsz].set(c_ctx)
    mods = _ada(cc, w_ada, b_ada).reshape(DEPTH, MOD_ROWS, 6, d)

    xf = x.reshape(n_lat, d)
    cf = ctx.reshape(n_ctx, d)
    lat_mod = lambda i: i // (seq // tm)
    ctx_mod = lambda i: bsz
    zero_state = jnp.zeros((bsz, GLA_HEADS, GLA_DV, GLA_DK), F32)

    for l in range(DEPTH):
        last = l == DEPTH - 1
        mod = mods[l]
        wi = w_in[l].astype(BF16)
        edges = [0]
        for s in (GLA_K, GLA_K, GLA_V, GLA_V, GLA_LR, GLA_LR, CONV_W, CONV_W, POOL_W, N_BRANCH * d):
            edges.append(edges[-1] + s)
        seg = lambda i, j=None: wi[:, edges[i]:edges[i + 1 if j is None else j]]
        w_lr = jnp.pad(seg(4, 6), ((0, 0), (0, LR_PAD - 2 * GLA_LR)))
        pre_w = (seg(0), seg(1), seg(2), seg(3), w_lr, seg(6), seg(7), seg(8))
        conv_p = (w_dw[l], row(b_dw[l]), row(g_conv_ln[l]), row(b_conv_ln[l]))
        wd = jnp.zeros((LR_PAD, 2 * GLA_K), F32)
        wd = wd.at[:GLA_LR, :GLA_K].set(w_decay[l, 0]).at[GLA_LR:2 * GLA_LR, GLA_K:].set(w_decay[l, 1])
        wd = wd.astype(BF16)
        bd = b_decay[l].reshape(1, 2 * GLA_K)
        mix_w = (seg(9), b_gate[l].reshape(1, N_BRANCH * d), row(g_gla[l]), w_gla_o[l].astype(BF16),
                 w_conv_o[l].astype(BF16), w_pool_g[l].astype(BF16), row(s_pool[l]),
                 w_pool_o[l].astype(BF16), w_out[l].astype(BF16))
        w1, w2 = w_mlp1[l].astype(BF16), w_mlp2[l].astype(BF16)
        gpm, gqm = row(g_pre_mix[l]), row(g_post_mix[l])
        gpf, gqf = row(g_pre_mlp[l]), row(g_post_mlp[l])

        q, k, v, sg, lr, cv, up = _pre(xf, mod, gpm, pre_w, conv_p, tm=tm, row_len=GRID_W, mod_index=lat_mod)
        cq, ck, cvv, csg, clr, ccv, cup = _pre(cf, mod, gpm, pre_w, conv_p, tm=ctx_len, row_len=ctx_len,
                                               mod_index=ctx_mod)
        co_f, co_b, cs_f, cs_b = _gla(cq, ck, cvv, clr, wd, bd, zero_state, zero_state, batch=bsz, t=ctx_len)
        o_f, o_b, _, _ = _gla(q, k, v, lr, wd, bd, cs_f, cs_b, batch=bsz, t=256)

        r4 = lambda a: a.reshape(bsz, rows, GRID_W, a.shape[-1])
        x1 = _mix_latent(r4(xf), mod, gpm, gqm, r4(o_f), r4(o_b), r4(sg), r4(cv), r4(up), mix_w, rb=32, cb=16)
        xf = _mlp(x1.reshape(n_lat, d), mod, gpf, gqf, w1, w2, tm=tm, mod_index=lat_mod)
        if not last:
            c1 = _mix_ctx(cf, mod, gpm, gqm, co_f, co_b, csg, ccv, cup, mix_w, batch=bsz)
            cf = _mlp(c1, mod, gpf, gqf, w1, w2, tm=tm, mod_index=ctx_mod)
    return xf.reshape(bsz, seq, d)
```

```python
import functools

import jax
import jax.numpy as jnp
from jax import lax
from jax.experimental import pallas as pl
from jax.experimental.pallas import tpu as pltpu

D_MODEL = 1024
DEPTH = 2
GRID_W = 64
GLA_HEADS = 4
GLA_DK = 128
GLA_DV = 256
GLA_K = GLA_HEADS * GLA_DK
GLA_V = GLA_HEADS * GLA_DV
GLA_LR = 16
GLA_TAU = 16.0
GLA_CHUNK = 64
GLA_BLOCK = 2 * GLA_CHUNK
CONV_W = 512
CONV_K = 31
CONV_HALO = 16
POOL_GROUPS = 4
POOL_W = 512
POOL_GC = 128
POOL_WINDOWS = (2, 4, 8, 16)
N_BRANCH = 3
D_FF = 4 * D_MODEL
EPS = 1e-6
LOG2E = 1.4426950408889634
LANES = 128
LR_PAD = LANES
MOD_ROWS = 16
VMEM_LIMIT = 56 * 1024 * 1024

F32 = jnp.float32
BF16 = jnp.bfloat16


def _dot(a, b):
    return jnp.dot(a, b, preferred_element_type=F32)


def _dot_nt(a, b):
    return lax.dot_general(a, b, (((1,), (1,)), ((), ())), preferred_element_type=F32)


def _dot_tn(a, b):
    return lax.dot_general(a, b, (((0,), (0,)), ((), ())), preferred_element_type=F32)


def _sigmoid(x):
    return jax.nn.sigmoid(x)


def _silu(x):
    return x * _sigmoid(x)


def _rms(x):
    return x * lax.rsqrt(jnp.mean(x * x, axis=-1, keepdims=True) + EPS)


def _cparams(sem, flags=None):
    return pltpu.CompilerParams(dimension_semantics=sem, vmem_limit_bytes=VMEM_LIMIT, flags=flags)


def _full(shape):
    nd = len(shape)
    return pl.BlockSpec(shape, lambda *_: (0,) * nd)


def _ada_kernel(cc_ref, w_ref, b_ref, o_ref):
    a = _silu(cc_ref[...]).astype(BF16)
    o_ref[0] = _dot(a, w_ref[0].astype(BF16)) + b_ref[0]


def _ada(cc, w_ada, b_ada):
    tn = 512
    n = 6 * D_MODEL
    return pl.pallas_call(
        _ada_kernel,
        grid=(DEPTH, n // tn),
        in_specs=[
            pl.BlockSpec((MOD_ROWS, D_MODEL), lambda l, j: (0, 0)),
            pl.BlockSpec((1, D_MODEL, tn), lambda l, j: (l, 0, j)),
            pl.BlockSpec((1, 1, tn), lambda l, j: (l, 0, j)),
        ],
        out_specs=pl.BlockSpec((1, MOD_ROWS, tn), lambda l, j: (l, 0, j)),
        out_shape=jax.ShapeDtypeStruct((DEPTH, MOD_ROWS, n), F32),
        compiler_params=_cparams(("parallel", "parallel")),
        name="ada",
    )(cc, w_ada, b_ada.reshape(DEPTH, 1, n))


def _pre_kernel(x_ref, mod_ref, g_ref, wq_ref, wk_ref, wv_ref, wg_ref, wlr_ref, wga_ref, wgb_ref, wp_ref,
                wdw_ref, bdw_ref, gln_ref, bln_ref,
                q_ref, k_ref, v_ref, sg_ref, lr_ref, cv_ref, up_ref, *, row_len):
    tm = x_ref.shape[0]
    h = _rms(x_ref[...]) * g_ref[...]
    h = h * (1.0 + mod_ref[0, 1:2, :]) + mod_ref[0, 0:1, :]
    hb = h.astype(BF16)
    u = _dot(hb, wga_ref[...]) * _sigmoid(_dot(hb, wgb_ref[...]))

    wp = row_len + 2 * CONV_HALO
    zeros = jnp.zeros((CONV_HALO, LANES), F32)
    row_blocks = []
    for r in range(tm // row_len):
        lane_blocks = []
        for c in range(CONV_W // LANES):
            cs = slice(c * LANES, (c + 1) * LANES)
            s = jnp.concatenate([zeros, u[r * row_len:(r + 1) * row_len, cs], zeros], axis=0)
            acc = None
            for j in range(CONV_K):
                d = j - CONV_K // 2
                sh = s if d == 0 else pltpu.roll(s, (wp - d) % wp, axis=0)
                term = sh[CONV_HALO:CONV_HALO + row_len, :] * wdw_ref[j:j + 1, cs]
                acc = term if acc is None else acc + term
            lane_blocks.append(acc + bdw_ref[:, cs])
        row_blocks.append(jnp.concatenate(lane_blocks, axis=1))
    y = jnp.concatenate(row_blocks, axis=0) if len(row_blocks) > 1 else row_blocks[0]
    yc = y - jnp.mean(y, axis=-1, keepdims=True)
    yn = yc * lax.rsqrt(jnp.mean(yc * yc, axis=-1, keepdims=True) + EPS) * gln_ref[...] + bln_ref[...]
    cv_ref[...] = _silu(yn).astype(BF16)

    q_ref[...] = (_dot(hb, wq_ref[...]) * (GLA_DK ** -0.5)).astype(BF16)
    k_ref[...] = _dot(hb, wk_ref[...]).astype(BF16)
    v_ref[...] = _dot(hb, wv_ref[...]).astype(BF16)
    sg_ref[...] = _silu(_dot(hb, wg_ref[...])).astype(BF16)
    lr_ref[...] = _dot(hb, wlr_ref[...]).astype(BF16)
    up_ref[...] = _dot(hb, wp_ref[...]).astype(BF16)


def _pre(xf, mod, g, w, conv, *, tm, row_len, mod_index):
    n = xf.shape[0]
    tok = lambda c: pl.BlockSpec((tm, c), lambda i: (i, 0))
    out_cols = (GLA_K, GLA_K, GLA_V, GLA_V, LR_PAD, CONV_W, POOL_W)
    return pl.pallas_call(
        functools.partial(_pre_kernel, row_len=row_len),
        grid=(n // tm,),
        in_specs=[tok(D_MODEL),
                  pl.BlockSpec((1, 6, D_MODEL), lambda i: (mod_index(i), 0, 0)),
                  _full((1, D_MODEL))]
                 + [_full(a.shape) for a in w]
                 + [_full(a.shape) for a in conv],
        out_specs=[tok(c) for c in out_cols],
        out_shape=[jax.ShapeDtypeStruct((n, c), BF16) for c in out_cols],
        compiler_params=_cparams(("parallel",)),
        name="pre",
    )(xf, mod, g, *w, *conv)


def _gla_kernel(qf_ref, kf_ref, vf_ref, lrf_ref, qb_ref, kb_ref, vb_ref, lrb_ref, wd_ref, bd_ref,
                s0f_ref, s0b_ref, of_ref, ob_ref, sf_ref, sb_ref):
    t = qf_ref.shape[0]
    nb = t // GLA_BLOCK

    @pl.when(pl.program_id(1) == 0)
    def _():
        sf_ref[...] = s0f_ref[...]
        sb_ref[...] = s0b_ref[...]

    brow = lax.broadcasted_iota(jnp.int32, (GLA_BLOCK, GLA_BLOCK), 0)
    bcol = lax.broadcasted_iota(jnp.int32, (GLA_BLOCK, GLA_BLOCK), 1)
    masks = (bcol <= brow, bcol >= brow)
    tris = tuple(jnp.where(m, 1.0, 0.0).astype(BF16) for m in masks)
    dirs = ((qf_ref, kf_ref, vf_ref, lrf_ref, of_ref, sf_ref),
            (qb_ref, kb_ref, vb_ref, lrb_ref, ob_ref, sb_ref))

    units = [(d, blk) for blk in range(nb) for d in (0, 1)]
    units = [(d, blk if d == 0 else nb - 1 - blk) for d, blk in units]
    rows = lambda blk: slice(blk * GLA_BLOCK, (blk + 1) * GLA_BLOCK)
    kcol = lambda h: slice(h * GLA_DK, (h + 1) * GLA_DK)
    vcol = lambda h: slice(h * GLA_DV, (h + 1) * GLA_DV)
    heads = range(GLA_HEADS)

    zs = [_dot(dirs[d][3][rows(blk), :], wd_ref[:, d * GLA_K:(d + 1) * GLA_K])
          + bd_ref[:, d * GLA_K:(d + 1) * GLA_K] for d, blk in units]
    l2s = [(jnp.minimum(z, 0.0) * LOG2E - jnp.log2(1.0 + jnp.exp2(jnp.abs(z) * -LOG2E))) * (1.0 / GLA_TAU)
           for z in zs]
    his = [l2.astype(BF16) for l2 in l2s]
    los = [(l2 - hi.astype(F32)).astype(BF16) for l2, hi in zip(l2s, his)]
    tri2 = [jnp.concatenate([tr, tr], axis=1) for tr in tris]
    cbs = [_dot(tri2[d], jnp.concatenate([hi, lo], axis=0)) for (d, _), hi, lo in zip(units, his, los)]
    ops = []
    for (d, blk), cb in zip(units, cbs):
        mid = GLA_CHUNK - 1 if d == 0 else GLA_CHUNK
        end = GLA_BLOCK - 1 if d == 0 else 0
        ref = cb[mid:mid + 1, :]
        last = cb[end:end + 1, :]
        e = jnp.exp2(cb - ref)
        qs = dirs[d][0][rows(blk), :].astype(F32) * e
        kr = dirs[d][1][rows(blk), :].astype(F32) * (1.0 / e)
        q2 = qs * jnp.exp2(ref)
        ke = kr * jnp.exp2(last - ref)
        gamma = jnp.exp2(last)
        ops.append((qs.astype(BF16), q2.astype(BF16), kr.astype(BF16), ke.astype(BF16), gamma))
    vs = [[dirs[d][2][rows(blk), vcol(h)] for h in heads] for d, blk in units]
    atts = [[jnp.where(masks[d], _dot_nt(op[0][:, kcol(h)], op[2][:, kcol(h)]), 0.0).astype(BF16)
             for h in heads] for (d, _), op in zip(units, ops)]
    incs = [[_dot_tn(op[3][:, kcol(h)], v[h]) for h in heads] for op, v in zip(ops, vs)]
    states = [[dirs[d][5][0, h] for h in heads] for d in (0, 1)]
    entering = []
    for (d, _), op, inc in zip(units, ops, incs):
        entering.append([st.astype(BF16) for st in states[d]])
        for h in heads:
            gcol = jnp.broadcast_to(op[4][:, kcol(h)], (GLA_DK, GLA_DK)).T
            states[d][h] = states[d][h] * jnp.concatenate([gcol] * (GLA_DV // GLA_DK), axis=1) + inc[h]
    for d in (0, 1):
        for h in heads:
            dirs[d][5][0, h] = states[d][h]
    for (d, blk), op, att, v, st in zip(units, ops, atts, vs, entering):
        for h in heads:
            o = _dot(jnp.concatenate([att[h], op[1][:, kcol(h)]], axis=1),
                     jnp.concatenate([v[h], st[h]], axis=0))
            dirs[d][4][rows(blk), vcol(h)] = o.astype(BF16)


def _gla(q, k, v, lr, wd, bd, s0f, s0b, *, batch, t):
    n = q.shape[0]
    nt = n // batch // t
    fwd = lambda c: pl.BlockSpec((t, c), lambda b, i: (b * nt + i, 0))
    bwd = lambda c: pl.BlockSpec((t, c), lambda b, i: (b * nt + nt - 1 - i, 0))
    st = pl.BlockSpec((1, GLA_HEADS, GLA_DK, GLA_DV), lambda b, i: (b, 0, 0, 0))
    st_shape = jax.ShapeDtypeStruct((batch, GLA_HEADS, GLA_DK, GLA_DV), F32)
    return pl.pallas_call(
        _gla_kernel,
        grid=(batch, nt),
        in_specs=[fwd(GLA_K), fwd(GLA_K), fwd(GLA_V), fwd(LR_PAD),
                  bwd(GLA_K), bwd(GLA_K), bwd(GLA_V), bwd(LR_PAD),
                  _full(wd.shape), _full(bd.shape), st, st],
        out_specs=[fwd(GLA_V), bwd(GLA_V), st, st],
        out_shape=[jax.ShapeDtypeStruct((n, GLA_V), BF16), jax.ShapeDtypeStruct((n, GLA_V), BF16),
                   st_shape, st_shape],
        compiler_params=_cparams(("parallel", "arbitrary")),
        name="gla",
    )(q, k, v, lr, q, k, v, lr, wd, bd, s0f, s0b)


def _window_counts(n, left, right, shape, axis):
    t = lax.broadcasted_iota(jnp.int32, shape, axis)
    lo = jnp.clip(t - left, 0, n)
    hi = jnp.clip(t + right + 1, 0, n)
    return (hi - lo).astype(F32)


def _mix_kernel(x_ref, mod_ref, gpre_ref, gpost_ref, of_ref, ob_ref, sg_ref, cv_ref, up_ref,
                wgt_ref, bgt_ref, ggla_ref, wgo_ref, wco_ref, wpg_ref, sp_ref, wpo_ref, wout_ref,
                o_ref, pd_scr, *, lead):
    tm = pd_scr.shape[0] if not lead else x_ref.shape[0] * x_ref.shape[1]
    ld = lambda ref: ref[...].reshape(tm, ref.shape[-1])

    @pl.when(pl.program_id(2) == 0)
    def _():
        up = up_ref[...].astype(F32)
        n = up.shape[0]
        for g, w in enumerate(POOL_WINDOWS):
            left, right = w // 2, w - 1 - w // 2
            ug = up[..., g * POOL_GC:(g + 1) * POOL_GC]
            total = ug
            for d in range(-left, right + 1):
                if d == 0:
                    continue
                if lead:
                    pad = jnp.zeros((abs(d),) + ug.shape[1:], F32)
                    sh = (jnp.concatenate([ug[d:], pad], axis=0) if d > 0
                          else jnp.concatenate([pad, ug[:n + d]], axis=0))
                else:
                    tpos = lax.broadcasted_iota(jnp.int32, ug.shape, 0)
                    sh = jnp.where((tpos + d >= 0) & (tpos + d < n),
                                   pltpu.roll(ug, (n - d) % n, axis=0), 0.0)
                total = total + sh
            cnt = _window_counts(n, left, right, (n,) + (1,) * (ug.ndim - 1), 0)
            diff = (total / cnt - ug).reshape(-1, POOL_GC).astype(BF16)
            cs = slice(g * POOL_GC, (g + 1) * POOL_GC)
            pd_scr[:, cs] = (_dot(diff, wpg_ref[g]) * sp_ref[:, cs]).astype(BF16)

    x = ld(x_ref)
    h = _rms(x) * gpre_ref[...]
    hb = (h * (1.0 + mod_ref[0, 1:2, :]) + mod_ref[0, 0:1, :]).astype(BF16)

    o = ld(of_ref).astype(F32) + ld(ob_ref).astype(F32)
    on = jnp.concatenate([_rms(o[:, hh * GLA_DV:(hh + 1) * GLA_DV]) for hh in range(GLA_HEADS)], axis=-1)
    a_in = (on * ggla_ref[...] * ld(sg_ref).astype(F32)).astype(BF16)
    ya = _dot(a_in, wgo_ref[...])
    gate = lambda i: _sigmoid(_dot(hb, wgt_ref[:, i * D_MODEL:(i + 1) * D_MODEL])
                              + bgt_ref[:, i * D_MODEL:(i + 1) * D_MODEL])
    mixed = gate(0) * ya
    mixed = mixed + gate(1) * _dot(ld(cv_ref), wco_ref[...])
    if lead:
        base = pl.multiple_of(pl.program_id(2) * tm, tm)
        pd = pd_scr[pl.ds(base, tm), :]
    else:
        pd = pd_scr[...]
    mixed = mixed + gate(2) * _dot(pd, wpo_ref[...])
    y = _dot(mixed.astype(BF16), wout_ref[...])
    out = x + mod_ref[0, 2:3, :] * (_rms(y) * gpost_ref[...])
    o_ref[...] = out.reshape(o_ref.shape)


def _mix_weight_specs(ws):
    return [_full(a.shape) for a in ws]


def _mix_latent(x4, mod, gpre, gpost, o_f, o_b, sg, cv, up, ws, *, rb, cb):
    bsz, rows, cols, _ = x4.shape
    blk = lambda c: pl.BlockSpec((None, rb, cb, c), lambda b, ci, ri: (b, ri, ci, 0))
    return pl.pallas_call(
        functools.partial(_mix_kernel, lead=True),
        grid=(bsz, cols // cb, rows // rb),
        in_specs=[blk(D_MODEL),
                  pl.BlockSpec((1, 6, D_MODEL), lambda b, ci, ri: (b, 0, 0)),
                  _full((1, D_MODEL)), _full((1, D_MODEL)),
                  blk(GLA_V), blk(GLA_V), blk(GLA_V), blk(CONV_W),
                  pl.BlockSpec((None, rows, cb, POOL_W), lambda b, ci, ri: (b, 0, ci, 0))]
                 + _mix_weight_specs(ws),
        out_specs=blk(D_MODEL),
        out_shape=jax.ShapeDtypeStruct(x4.shape, F32),
        scratch_shapes=[pltpu.VMEM((rows * cb, POOL_W), BF16)],
        compiler_params=_cparams(("parallel", "parallel", "arbitrary")),
        name="mix",
    )(x4, mod, gpre, gpost, o_f, o_b, sg, cv, up, *ws)


def _mix_ctx(xf, mod, gpre, gpost, o_f, o_b, sg, cv, up, ws, *, batch):
    n = xf.shape[0]
    tm = n // batch
    blk = lambda c: pl.BlockSpec((tm, c), lambda b, ci, ri: (b, 0))
    return pl.pallas_call(
        functools.partial(_mix_kernel, lead=False),
        grid=(batch, 1, 1),
        in_specs=[blk(D_MODEL),
                  pl.BlockSpec((1, 6, D_MODEL), lambda b, ci, ri: (batch, 0, 0)),
                  _full((1, D_MODEL)), _full((1, D_MODEL)),
                  blk(GLA_V), blk(GLA_V), blk(GLA_V), blk(CONV_W), blk(POOL_W)]
                 + _mix_weight_specs(ws),
        out_specs=blk(D_MODEL),
        out_shape=jax.ShapeDtypeStruct(xf.shape, F32),
        scratch_shapes=[pltpu.VMEM((tm, POOL_W), BF16)],
        compiler_params=_cparams(("parallel", "parallel", "arbitrary")),
        name="mix_ctx",
    )(xf, mod, gpre, gpost, o_f, o_b, sg, cv, up, *ws)


def _mlp_kernel(x_ref, mod_ref, gpre_ref, gpost_ref, w1_ref, w2_ref, o_ref):
    x = x_ref[...]
    h = _rms(x) * gpre_ref[...]
    hb = (h * (1.0 + mod_ref[0, 4:5, :]) + mod_ref[0, 3:4, :]).astype(BF16)
    y = jnp.zeros(x.shape, F32)
    for c in range(D_FF // D_MODEL):
        cs = slice(c * D_MODEL, (c + 1) * D_MODEL)
        a = jnp.maximum(_dot(hb, w1_ref[:, cs]), 0.0)
        y = y + _dot((a * a).astype(BF16), w2_ref[cs, :])
    o_ref[...] = x + mod_ref[0, 5:6, :] * (_rms(y) * gpost_ref[...])


def _mlp(xf, mod, gpre, gpost, w1, w2, *, tm, mod_index):
    n = xf.shape[0]
    tok = pl.BlockSpec((tm, D_MODEL), lambda i: (i, 0))
    return pl.pallas_call(
        _mlp_kernel,
        grid=(n // tm,),
        in_specs=[tok, pl.BlockSpec((1, 6, D_MODEL), lambda i: (mod_index(i), 0, 0)),
                  _full((1, D_MODEL)), _full((1, D_MODEL)), _full(w1.shape), _full(w2.shape)],
        out_specs=tok,
        out_shape=jax.ShapeDtypeStruct(xf.shape, F32),
        compiler_params=_cparams(("parallel",)),
        name="mlp",
    )(xf, mod, gpre, gpost, w1, w2)


def kernel(x, c, ctx, c_ctx, w_ada, b_ada, g_pre_mix, g_post_mix, g_pre_mlp, g_post_mlp, w_in, w_decay,
           b_decay, g_gla, w_gla_o, w_dw, b_dw, g_conv_ln, b_conv_ln, w_conv_o, w_pool_g, s_pool,
           w_pool_o, b_gate, w_out, w_mlp1, w_mlp2):
    bsz, seq, d = x.shape
    ctx_len = ctx.shape[1]
    rows = seq // GRID_W
    n_lat, n_ctx = bsz * seq, bsz * ctx_len
    tm = 512
    row = lambda a: a.reshape(1, -1)

    cc = jnp.zeros((MOD_ROWS, d), F32).at[:bsz].set(c).at[bsz].set(c_ctx)
    mods = _ada(cc, w_ada, b_ada).reshape(DEPTH, MOD_ROWS, 6, d)

    xf = x.reshape(n_lat, d)
    cf = ctx.reshape(n_ctx, d)
    lat_mod = lambda i: i // (seq // tm)
    ctx_mod = lambda i: bsz
    zero_state = jnp.zeros((bsz, GLA_HEADS, GLA_DK, GLA_DV), F32)

    for l in range(DEPTH):
        last = l == DEPTH - 1
        mod = mods[l]
        wi = w_in[l].astype(BF16)
        edges = [0]
        for s in (GLA_K, GLA_K, GLA_V, GLA_V, GLA_LR, GLA_LR, CONV_W, CONV_W, POOL_W, N_BRANCH * d):
            edges.append(edges[-1] + s)
        seg = lambda i, j=None: wi[:, edges[i]:edges[i + 1 if j is None else j]]
        w_lr = jnp.pad(seg(4, 6), ((0, 0), (0, LR_PAD - 2 * GLA_LR)))
        pre_w = (seg(0), seg(1), seg(2), seg(3), w_lr, seg(6), seg(7), seg(8))
        conv_p = (w_dw[l], row(b_dw[l]), row(g_conv_ln[l]), row(b_conv_ln[l]))
        wd = jnp.zeros((LR_PAD, 2 * GLA_K), F32)
        wd = wd.at[:GLA_LR, :GLA_K].set(w_decay[l, 0]).at[GLA_LR:2 * GLA_LR, GLA_K:].set(w_decay[l, 1])
        wd = wd.astype(BF16)
        bd = b_decay[l].reshape(1, 2 * GLA_K)
        mix_w = (seg(9), b_gate[l].reshape(1, N_BRANCH * d), row(g_gla[l]), w_gla_o[l].astype(BF16),
                 w_conv_o[l].astype(BF16), w_pool_g[l].astype(BF16), row(s_pool[l]),
                 w_pool_o[l].astype(BF16), w_out[l].astype(BF16))
        w1, w2 = w_mlp1[l].astype(BF16), w_mlp2[l].astype(BF16)
        gpm, gqm = row(g_pre_mix[l]), row(g_post_mix[l])
        gpf, gqf = row(g_pre_mlp[l]), row(g_post_mlp[l])

        q, k, v, sg, lr, cv, up = _pre(xf, mod, gpm, pre_w, conv_p, tm=tm, row_len=GRID_W, mod_index=lat_mod)
        cq, ck, cvv, csg, clr, ccv, cup = _pre(cf, mod, gpm, pre_w, conv_p, tm=ctx_len, row_len=ctx_len,
                                               mod_index=ctx_mod)
        co_f, co_b, cs_f, cs_b = _gla(cq, ck, cvv, clr, wd, bd, zero_state, zero_state, batch=bsz, t=ctx_len)
        o_f, o_b, _, _ = _gla(q, k, v, lr, wd, bd, cs_f, cs_b, batch=bsz, t=512)

        r4 = lambda a: a.reshape(bsz, rows, GRID_W, a.shape[-1])
        x1 = _mix_latent(r4(xf), mod, gpm, gqm, r4(o_f), r4(o_b), r4(sg), r4(cv), r4(up), mix_w, rb=32, cb=16)
        xf = _mlp(x1.reshape(n_lat, d), mod, gpf, gqf, w1, w2, tm=tm, mod_index=lat_mod)
        if not last:
            c1 = _mix_ctx(cf, mod, gpm, gqm, co_f, co_b, csg, ccv, cup, mix_w, batch=bsz)
            cf = _mlp(c1, mod, gpf, gqf, w1, w2, tm=tm, mod_index=ctx_mod)
    return xf.reshape(bsz, seq, d)
```

```python
import functools

import jax
import jax.numpy as jnp
from jax import lax
from jax.experimental import pallas as pl
from jax.experimental.pallas import tpu as pltpu

D_MODEL = 1024
DEPTH = 2
GRID_W = 64
GLA_HEADS = 4
GLA_DK = 128
GLA_DV = 256
GLA_K = GLA_HEADS * GLA_DK
GLA_V = GLA_HEADS * GLA_DV
GLA_LR = 16
GLA_TAU = 16.0
GLA_CHUNK = 64
GLA_BLOCK = 2 * GLA_CHUNK
CONV_W = 512
CONV_K = 31
CONV_HALO = 16
POOL_GROUPS = 4
POOL_W = 512
POOL_GC = 128
POOL_WINDOWS = (2, 4, 8, 16)
N_BRANCH = 3
D_FF = 4 * D_MODEL
EPS = 1e-6
LOG2E = 1.4426950408889634
LANES = 128
SUBLANES = 8
LR_PAD = LANES
PRE_CHUNK = 512
PA_Q, PA_K, PA_V, PA_POOL = 0, 1, 2, 4
MOD_ROWS = 16
VMEM_LIMIT = 56 * 1024 * 1024

F32 = jnp.float32
BF16 = jnp.bfloat16


def _dot(a, b):
    return jnp.dot(a, b, preferred_element_type=F32)


def _dot_nt(a, b):
    return lax.dot_general(a, b, (((1,), (1,)), ((), ())), preferred_element_type=F32)


def _dot_tn(a, b):
    return lax.dot_general(a, b, (((0,), (0,)), ((), ())), preferred_element_type=F32)


def _sigmoid(x):
    return jax.nn.sigmoid(x)


def _silu(x):
    return x * _sigmoid(x)


def _rms(x):
    return x * lax.rsqrt(jnp.mean(x * x, axis=-1, keepdims=True) + EPS)


def _cparams(sem):
    return pltpu.CompilerParams(dimension_semantics=sem, vmem_limit_bytes=VMEM_LIMIT)


def _full(shape):
    nd = len(shape)
    return pl.BlockSpec(shape, lambda *_: (0,) * nd)


def _ada_kernel(cc_ref, w_ref, b_ref, o_ref):
    a = _silu(cc_ref[...]).astype(BF16)
    o_ref[0] = _dot(a, w_ref[0].astype(BF16)) + b_ref[0]


def _ada(cc, w_ada, b_ada):
    tn = 512
    n = 6 * D_MODEL
    return pl.pallas_call(
        _ada_kernel,
        grid=(DEPTH, n // tn),
        in_specs=[
            pl.BlockSpec((MOD_ROWS, D_MODEL), lambda l, j: (0, 0)),
            pl.BlockSpec((1, D_MODEL, tn), lambda l, j: (l, 0, j)),
            pl.BlockSpec((1, 1, tn), lambda l, j: (l, 0, j)),
        ],
        out_specs=pl.BlockSpec((1, MOD_ROWS, tn), lambda l, j: (l, 0, j)),
        out_shape=jax.ShapeDtypeStruct((DEPTH, MOD_ROWS, n), F32),
        compiler_params=_cparams(("parallel", "parallel")),
        name="ada",
    )(cc, w_ada, b_ada.reshape(DEPTH, 1, n))


def _pre_kernel(x_ref, mod_ref, g_ref, wa_ref, wb_ref, wlr_ref, wga_ref, wgb_ref,
                wdw_ref, bdw_ref, gln_ref, bln_ref,
                pa_ref, pb_ref, lr_ref, cv_ref, hb_scr, u_scr, y_scr, *, row_len):
    tm = x_ref.shape[0]
    h = _rms(x_ref[...]) * g_ref[...]
    h = h * (1.0 + mod_ref[0, 1:2, :]) + mod_ref[0, 0:1, :]
    hb_scr[...] = h.astype(BF16)
    u_scr[...] = _dot(hb_scr[...], wga_ref[...]) * _sigmoid(_dot(hb_scr[...], wgb_ref[...]))

    wp = row_len + 2 * CONV_HALO
    zeros = jnp.zeros((CONV_HALO, LANES), F32)

    def conv_strip(base):
        for c in range(CONV_W // LANES):
            cs = slice(c * LANES, (c + 1) * LANES)
            s = jnp.concatenate([zeros, u_scr[pl.ds(base, row_len), cs], zeros], axis=0)
            acc = None
            for j in range(CONV_K):
                d = j - CONV_K // 2
                sh = s if d == 0 else pltpu.roll(s, (wp - d) % wp, axis=0)
                term = sh[CONV_HALO:CONV_HALO + row_len, :] * wdw_ref[j:j + 1, cs]
                acc = term if acc is None else acc + term
            y_scr[pl.ds(base, row_len), cs] = acc + bdw_ref[:, cs]

    def plain(i):
        pa_ref[i] = _dot(hb_scr[...], wa_ref[i]).astype(BF16)

    def gated(i):
        pb_ref[i] = _silu(_dot(hb_scr[...], wb_ref[i])).astype(BF16)

    n_plain, n_gated = wa_ref.shape[0], wb_ref.shape[0]
    n_strips = tm // row_len
    group = max(1, n_strips // (n_plain + n_gated + 1))
    n_loop = min((n_strips - 1) // group, n_plain + n_gated)
    na = min(n_loop, n_plain)
    nb = n_loop - na

    def conv_group(i):
        for g in range(group):
            conv_strip(pl.multiple_of((i * group + g) * row_len, row_len))

    def body_plain(i, carry):
        conv_group(i)
        plain(i)
        return carry

    def body_gated(i, carry):
        conv_group(na + i)
        gated(i)
        return carry

    if na > 0:
        lax.fori_loop(0, na, body_plain, 0)
    if nb > 0:
        lax.fori_loop(0, nb, body_gated, 0)
    for i in range(na, n_plain):
        plain(i)
    for i in range(nb, n_gated):
        gated(i)
    for r in range(n_loop * group, n_strips):
        conv_strip(r * row_len)
    lr_ref[...] = _dot(hb_scr[...], wlr_ref[...]).astype(BF16)

    y = y_scr[...]
    yc = y - jnp.mean(y, axis=-1, keepdims=True)
    yn = yc * lax.rsqrt(jnp.mean(yc * yc, axis=-1, keepdims=True) + EPS) * gln_ref[...] + bln_ref[...]
    cv_ref[...] = _silu(yn).astype(BF16)


def _pre(xf, mod, g, w, conv, *, tm, row_len, mod_index):
    n = xf.shape[0]
    wa, wb = w[0], w[1]
    tok = lambda c: pl.BlockSpec((tm, c), lambda i: (i, 0))
    stack = lambda k: pl.BlockSpec((k, tm, PRE_CHUNK), lambda i: (0, i, 0))
    return pl.pallas_call(
        functools.partial(_pre_kernel, row_len=row_len),
        grid=(n // tm,),
        in_specs=[tok(D_MODEL),
                  pl.BlockSpec((1, 6, D_MODEL), lambda i: (mod_index(i), 0, 0)),
                  _full((1, D_MODEL))]
                 + [_full(a.shape) for a in w]
                 + [_full(a.shape) for a in conv],
        out_specs=[stack(wa.shape[0]), stack(wb.shape[0]), tok(LR_PAD), tok(CONV_W)],
        out_shape=[jax.ShapeDtypeStruct((wa.shape[0], n, PRE_CHUNK), BF16),
                   jax.ShapeDtypeStruct((wb.shape[0], n, PRE_CHUNK), BF16),
                   jax.ShapeDtypeStruct((n, LR_PAD), BF16), jax.ShapeDtypeStruct((n, CONV_W), BF16)],
        scratch_shapes=[pltpu.VMEM((tm, D_MODEL), BF16), pltpu.VMEM((tm, CONV_W), F32),
                        pltpu.VMEM((tm, CONV_W), F32)],
        compiler_params=_cparams(("parallel",)),
        name="pre",
    )(xf, mod, g, *w, *conv)


def _gla_kernel(qf_ref, kf_ref, vf_ref, lrf_ref, qb_ref, kb_ref, vb_ref, lrb_ref, wd_ref, bd_ref,
                s0f_ref, s0b_ref, of_ref, ob_ref, sf_ref, sb_ref):
    t = qf_ref.shape[0]
    nb = t // GLA_BLOCK

    @pl.when(pl.program_id(1) == 0)
    def _():
        sf_ref[...] = s0f_ref[...]
        sb_ref[...] = s0b_ref[...]

    brow = lax.broadcasted_iota(jnp.int32, (GLA_BLOCK, GLA_BLOCK), 0)
    bcol = lax.broadcasted_iota(jnp.int32, (GLA_BLOCK, GLA_BLOCK), 1)
    masks = (bcol <= brow, bcol >= brow)
    tris = tuple(jnp.where(m, 1.0, 0.0).astype(BF16) for m in masks)
    dirs = ((qf_ref, kf_ref, vf_ref, lrf_ref, of_ref, sf_ref),
            (qb_ref, kb_ref, vb_ref, lrb_ref, ob_ref, sb_ref))

    units = [(d, blk) for blk in range(nb) for d in (0, 1)]
    units = [(d, blk if d == 0 else nb - 1 - blk) for d, blk in units]
    rows = lambda blk: slice(blk * GLA_BLOCK, (blk + 1) * GLA_BLOCK)
    kcol = lambda h: slice(h * GLA_DK, (h + 1) * GLA_DK)
    vcol = lambda h: slice(h * GLA_DV, (h + 1) * GLA_DV)
    heads = range(GLA_HEADS)

    zs = [_dot(dirs[d][3][rows(blk), :], wd_ref[:, d * GLA_K:(d + 1) * GLA_K])
          + bd_ref[:, d * GLA_K:(d + 1) * GLA_K] for d, blk in units]
    l2s = [(jnp.minimum(z, 0.0) * LOG2E - jnp.log2(1.0 + jnp.exp2(jnp.abs(z) * -LOG2E))) * (1.0 / GLA_TAU)
           for z in zs]
    his = [l2.astype(BF16) for l2 in l2s]
    los = [(l2 - hi.astype(F32)).astype(BF16) for l2, hi in zip(l2s, his)]
    tri2 = [jnp.concatenate([tr, tr], axis=1) for tr in tris]
    cbs = [_dot(tri2[d], jnp.concatenate([hi, lo], axis=0)) for (d, _), hi, lo in zip(units, his, los)]
    ops = []
    for (d, blk), cb in zip(units, cbs):
        mid = GLA_CHUNK - 1 if d == 0 else GLA_CHUNK
        end = GLA_BLOCK - 1 if d == 0 else 0
        ref = cb[mid:mid + 1, :]
        last = cb[end:end + 1, :]
        e = jnp.exp2(cb - ref)
        qs = dirs[d][0][rows(blk), :].astype(F32) * (e * GLA_DK ** -0.5)
        kr = dirs[d][1][rows(blk), :].astype(F32) * (1.0 / e)
        q2 = qs * jnp.exp2(ref)
        ke = kr * jnp.exp2(last - ref)
        gamma = jnp.exp2(last)
        ops.append((qs.astype(BF16), q2.astype(BF16), kr.astype(BF16), ke.astype(BF16), gamma))
    per_chunk = PRE_CHUNK // GLA_DV
    vs = [[dirs[d][2][h // per_chunk, rows(blk), (h % per_chunk) * GLA_DV:(h % per_chunk + 1) * GLA_DV]
           for h in heads] for d, blk in units]
    atts = [[jnp.where(masks[d], _dot_nt(op[0][:, kcol(h)], op[2][:, kcol(h)]), 0.0).astype(BF16)
             for h in heads] for (d, _), op in zip(units, ops)]
    incs = [[_dot_tn(op[3][:, kcol(h)], v[h]) for h in heads] for op, v in zip(ops, vs)]
    states = [[dirs[d][5][0, h] for h in heads] for d in (0, 1)]
    entering = []
    for (d, _), op, inc in zip(units, ops, incs):
        entering.append([st.astype(BF16) for st in states[d]])
        for h in heads:
            gcol = jnp.broadcast_to(op[4][:, kcol(h)], (GLA_DK, GLA_DK)).T
            states[d][h] = states[d][h] * jnp.concatenate([gcol] * (GLA_DV // GLA_DK), axis=1) + inc[h]
    for d in (0, 1):
        for h in heads:
            dirs[d][5][0, h] = states[d][h]
    for (d, blk), op, att, v, st in zip(units, ops, atts, vs, entering):
        for h in heads:
            o = _dot(jnp.concatenate([att[h], op[1][:, kcol(h)]], axis=1),
                     jnp.concatenate([v[h], st[h]], axis=0))
            dirs[d][4][rows(blk), vcol(h)] = o.astype(BF16)


def _gla(pa, lr, wd, bd, s0f, s0b, *, batch, t):
    n = pa.shape[1]
    nt = n // batch // t
    frow = lambda b, i: b * nt + i
    brow = lambda b, i: b * nt + nt - 1 - i
    tok = lambda c, row: pl.BlockSpec((t, c), lambda b, i: (row(b, i), 0))
    one = lambda chunk, row: pl.BlockSpec((None, t, PRE_CHUNK), lambda b, i: (chunk, row(b, i), 0))
    n_v = GLA_V // PRE_CHUNK
    val = lambda row: pl.BlockSpec((n_v, t, PRE_CHUNK), lambda b, i: (PA_V // n_v, row(b, i), 0))
    st = pl.BlockSpec((1, GLA_HEADS, GLA_DK, GLA_DV), lambda b, i: (b, 0, 0, 0))
    st_shape = jax.ShapeDtypeStruct((batch, GLA_HEADS, GLA_DK, GLA_DV), F32)
    return pl.pallas_call(
        _gla_kernel,
        grid=(batch, nt),
        in_specs=[one(PA_Q, frow), one(PA_K, frow), val(frow), tok(LR_PAD, frow),
                  one(PA_Q, brow), one(PA_K, brow), val(brow), tok(LR_PAD, brow),
                  _full(wd.shape), _full(bd.shape), st, st],
        out_specs=[tok(GLA_V, frow), tok(GLA_V, brow), st, st],
        out_shape=[jax.ShapeDtypeStruct((n, GLA_V), BF16), jax.ShapeDtypeStruct((n, GLA_V), BF16),
                   st_shape, st_shape],
        compiler_params=_cparams(("parallel", "arbitrary")),
        name="gla",
    )(pa, pa, pa, lr, pa, pa, pa, lr, wd, bd, s0f, s0b)


def _window_counts(n, left, right, shape, axis):
    t = lax.broadcasted_iota(jnp.int32, shape, axis)
    lo = jnp.clip(t - left, 0, n)
    hi = jnp.clip(t + right + 1, 0, n)
    return (hi - lo).astype(F32)


def _mix_kernel(x_ref, mod_ref, gpre_ref, gpost_ref, of_ref, ob_ref, sg_ref, cv_ref, up_ref,
                wgt_ref, bgt_ref, ggla_ref, wgo_ref, wco_ref, wpg_ref, sp_ref, wpo_ref, wout_ref,
                o_ref, pd_scr, *, lead):
    tm = pd_scr.shape[0] if not lead else x_ref.shape[0] * x_ref.shape[1]
    ld = lambda ref: ref[...].reshape(tm, ref.shape[-1])

    @pl.when(pl.program_id(2) == 0)
    def _():
        up = up_ref[...].astype(F32)
        n = up.shape[0]
        for g, w in enumerate(POOL_WINDOWS):
            left, right = w // 2, w - 1 - w // 2
            ug = up[..., g * POOL_GC:(g + 1) * POOL_GC]
            total = ug
            for d in range(-left, right + 1):
                if d == 0:
                    continue
                if lead:
                    pad = jnp.zeros((abs(d),) + ug.shape[1:], F32)
                    sh = (jnp.concatenate([ug[d:], pad], axis=0) if d > 0
                          else jnp.concatenate([pad, ug[:n + d]], axis=0))
                else:
                    tpos = lax.broadcasted_iota(jnp.int32, ug.shape, 0)
                    sh = jnp.where((tpos + d >= 0) & (tpos + d < n),
                                   pltpu.roll(ug, (n - d) % n, axis=0), 0.0)
                total = total + sh
            cnt = _window_counts(n, left, right, (n,) + (1,) * (ug.ndim - 1), 0)
            diff = (total / cnt - ug).reshape(-1, POOL_GC).astype(BF16)
            cs = slice(g * POOL_GC, (g + 1) * POOL_GC)
            pd_scr[:, cs] = (_dot(diff, wpg_ref[g]) * sp_ref[:, cs]).astype(BF16)

    x = ld(x_ref)
    h = _rms(x) * gpre_ref[...]
    hb = (h * (1.0 + mod_ref[0, 1:2, :]) + mod_ref[0, 0:1, :]).astype(BF16)

    o = ld(of_ref).astype(F32) + ld(ob_ref).astype(F32)
    on = jnp.concatenate([_rms(o[:, hh * GLA_DV:(hh + 1) * GLA_DV]) for hh in range(GLA_HEADS)], axis=-1)
    sg = jnp.concatenate([sg_ref[i].reshape(tm, PRE_CHUNK) for i in range(sg_ref.shape[0])], axis=1)
    a_in = (on * ggla_ref[...] * sg.astype(F32)).astype(BF16)
    ya = _dot(a_in, wgo_ref[...])
    gate = lambda i: _sigmoid(_dot(hb, wgt_ref[:, i * D_MODEL:(i + 1) * D_MODEL])
                              + bgt_ref[:, i * D_MODEL:(i + 1) * D_MODEL])
    mixed = gate(0) * ya
    mixed = mixed + gate(1) * _dot(ld(cv_ref), wco_ref[...])
    if lead:
        base = pl.multiple_of(pl.program_id(2) * tm, tm)
        pd = pd_scr[pl.ds(base, tm), :]
    else:
        pd = pd_scr[...]
    mixed = mixed + gate(2) * _dot(pd, wpo_ref[...])
    y = _dot(mixed.astype(BF16), wout_ref[...])
    out = x + mod_ref[0, 2:3, :] * (_rms(y) * gpost_ref[...])
    o_ref[...] = out.reshape(o_ref.shape)


def _mix_weight_specs(ws):
    return [_full(a.shape) for a in ws]


def _mix_latent(x4, mod, gpre, gpost, o_f, o_b, sg, cv, up, ws, *, rb, cb):
    bsz, rows, cols, _ = x4.shape
    blk = lambda c: pl.BlockSpec((None, rb, cb, c), lambda b, ci, ri: (b, ri, ci, 0))
    return pl.pallas_call(
        functools.partial(_mix_kernel, lead=True),
        grid=(bsz, cols // cb, rows // rb),
        in_specs=[blk(D_MODEL),
                  pl.BlockSpec((1, 6, D_MODEL), lambda b, ci, ri: (b, 0, 0)),
                  _full((1, D_MODEL)), _full((1, D_MODEL)),
                  blk(GLA_V), blk(GLA_V),
                  pl.BlockSpec((sg.shape[0], None, rb, cb, PRE_CHUNK), lambda b, ci, ri: (0, b, ri, ci, 0)),
                  blk(CONV_W),
                  pl.BlockSpec((None, None, rows, cb, POOL_W), lambda b, ci, ri: (PA_POOL, b, 0, ci, 0))]
                 + _mix_weight_specs(ws),
        out_specs=blk(D_MODEL),
        out_shape=jax.ShapeDtypeStruct(x4.shape, F32),
        scratch_shapes=[pltpu.VMEM((rows * cb, POOL_W), BF16)],
        compiler_params=_cparams(("parallel", "parallel", "arbitrary")),
        name="mix",
    )(x4, mod, gpre, gpost, o_f, o_b, sg, cv, up, *ws)


def _mix_ctx(xf, mod, gpre, gpost, o_f, o_b, sg, cv, up, ws, *, batch):
    n = xf.shape[0]
    tm = n // batch
    blk = lambda c: pl.BlockSpec((tm, c), lambda b, ci, ri: (b, 0))
    return pl.pallas_call(
        functools.partial(_mix_kernel, lead=False),
        grid=(batch, 1, 1),
        in_specs=[blk(D_MODEL),
                  pl.BlockSpec((1, 6, D_MODEL), lambda b, ci, ri: (batch, 0, 0)),
                  _full((1, D_MODEL)), _full((1, D_MODEL)),
                  blk(GLA_V), blk(GLA_V),
                  pl.BlockSpec((sg.shape[0], tm, PRE_CHUNK), lambda b, ci, ri: (0, b, 0)),
                  blk(CONV_W),
                  pl.BlockSpec((None, tm, POOL_W), lambda b, ci, ri: (PA_POOL, b, 0))]
                 + _mix_weight_specs(ws),
        out_specs=blk(D_MODEL),
        out_shape=jax.ShapeDtypeStruct(xf.shape, F32),
        scratch_shapes=[pltpu.VMEM((tm, POOL_W), BF16)],
        compiler_params=_cparams(("parallel", "parallel", "arbitrary")),
        name="mix_ctx",
    )(xf, mod, gpre, gpost, o_f, o_b, sg, cv, up, *ws)


def _mlp_kernel(x_ref, mod_ref, gpre_ref, gpost_ref, w1_ref, w2_ref, o_ref):
    x = x_ref[...]
    h = _rms(x) * gpre_ref[...]
    hb = (h * (1.0 + mod_ref[0, 4:5, :]) + mod_ref[0, 3:4, :]).astype(BF16)
    y = jnp.zeros(x.shape, F32)
    for c in range(D_FF // D_MODEL):
        cs = slice(c * D_MODEL, (c + 1) * D_MODEL)
        a = jnp.maximum(_dot(hb, w1_ref[:, cs]), 0.0)
        y = y + _dot((a * a).astype(BF16), w2_ref[cs, :])
    o_ref[...] = x + mod_ref[0, 5:6, :] * (_rms(y) * gpost_ref[...])


def _mlp(xf, mod, gpre, gpost, w1, w2, *, tm, mod_index):
    n = xf.shape[0]
    tok = pl.BlockSpec((tm, D_MODEL), lambda i: (i, 0))
    return pl.pallas_call(
        _mlp_kernel,
        grid=(n // tm,),
        in_specs=[tok, pl.BlockSpec((1, 6, D_MODEL), lambda i: (mod_index(i), 0, 0)),
                  _full((1, D_MODEL)), _full((1, D_MODEL)), _full(w1.shape), _full(w2.shape)],
        out_specs=tok,
        out_shape=jax.ShapeDtypeStruct(xf.shape, F32),
        compiler_params=_cparams(("parallel",)),
        name="mlp",
    )(xf, mod, gpre, gpost, w1, w2)


def kernel(x, c, ctx, c_ctx, w_ada, b_ada, g_pre_mix, g_post_mix, g_pre_mlp, g_post_mlp, w_in, w_decay,
           b_decay, g_gla, w_gla_o, w_dw, b_dw, g_conv_ln, b_conv_ln, w_conv_o, w_pool_g, s_pool,
           w_pool_o, b_gate, w_out, w_mlp1, w_mlp2):
    bsz, seq, d = x.shape
    ctx_len = ctx.shape[1]
    rows = seq // GRID_W
    n_lat, n_ctx = bsz * seq, bsz * ctx_len
    tm = 1024
    row = lambda a: a.reshape(1, -1)

    cc = jnp.zeros((MOD_ROWS, d), F32).at[:bsz].set(c).at[bsz].set(c_ctx)
    mods = _ada(cc, w_ada, b_ada).reshape(DEPTH, MOD_ROWS, 6, d)

    xf = x.reshape(n_lat, d)
    cf = ctx.reshape(n_ctx, d)
    lat_mod = lambda i: i // (seq // tm)
    ctx_mod = lambda i: bsz
    zero_state = jnp.zeros((bsz, GLA_HEADS, GLA_DK, GLA_DV), F32)

    for l in range(DEPTH):
        last = l == DEPTH - 1
        mod = mods[l]
        edges = [0]
        for s in (GLA_K, GLA_K, GLA_V, GLA_V, GLA_LR, GLA_LR, CONV_W, CONV_W, POOL_W, N_BRANCH * d):
            edges.append(edges[-1] + s)
        seg = lambda i, j=None: w_in[l, :, edges[i]:edges[i + 1 if j is None else j]].astype(BF16)
        w_lr = jnp.pad(seg(4, 6), ((0, 0), (0, LR_PAD - 2 * GLA_LR)))
        chunks = lambda a: [a[:, c:c + PRE_CHUNK] for c in range(0, a.shape[1], PRE_CHUNK)]
        w_plain = jnp.stack(chunks(seg(0)) + chunks(seg(1)) + chunks(seg(2)) + chunks(seg(8)))
        w_gated = jnp.stack(chunks(seg(3)))
        pre_w = (w_plain, w_gated, w_lr, seg(6), seg(7))
        conv_p = (w_dw[l], row(b_dw[l]), row(g_conv_ln[l]), row(b_conv_ln[l]))
        wd = jnp.zeros((LR_PAD, 2 * GLA_K), F32)
        wd = wd.at[:GLA_LR, :GLA_K].set(w_decay[l, 0]).at[GLA_LR:2 * GLA_LR, GLA_K:].set(w_decay[l, 1])
        wd = wd.astype(BF16)
        bd = b_decay[l].reshape(1, 2 * GLA_K)
        mix_w = (seg(9), b_gate[l].reshape(1, N_BRANCH * d), row(g_gla[l]), w_gla_o[l].astype(BF16),
                 w_conv_o[l].astype(BF16), w_pool_g[l].astype(BF16), row(s_pool[l]),
                 w_pool_o[l].astype(BF16), w_out[l].astype(BF16))
        w1, w2 = w_mlp1[l].astype(BF16), w_mlp2[l].astype(BF16)
        gpm, gqm = row(g_pre_mix[l]), row(g_post_mix[l])
        gpf, gqf = row(g_pre_mlp[l]), row(g_post_mlp[l])

        pa, pb, lr, cv = _pre(xf, mod, gpm, pre_w, conv_p, tm=tm, row_len=GRID_W, mod_index=lat_mod)
        cpa, cpb, clr, ccv = _pre(cf, mod, gpm, pre_w, conv_p, tm=ctx_len, row_len=ctx_len, mod_index=ctx_mod)
        co_f, co_b, cs_f, cs_b = _gla(cpa, clr, wd, bd, zero_state, zero_state, batch=bsz, t=ctx_len)
        o_f, o_b, _, _ = _gla(pa, lr, wd, bd, cs_f, cs_b, batch=bsz, t=512)

        r4 = lambda a: a.reshape(a.shape[:-2] + (bsz, rows, GRID_W, a.shape[-1]))
        x1 = _mix_latent(r4(xf), mod, gpm, gqm, r4(o_f), r4(o_b), r4(pb), r4(cv), r4(pa), mix_w, rb=32, cb=16)
        xf = _mlp(x1.reshape(n_lat, d), mod, gpf, gqf, w1, w2, tm=tm, mod_index=lat_mod)
        if not last:
            c1 = _mix_ctx(cf, mod, gpm, gqm, co_f, co_b, cpb, ccv, cpa, mix_w, batch=bsz)
            cf = _mlp(c1, mod, gpf, gqf, w1, w2, tm=min(tm, n_ctx), mod_index=ctx_mod)
    return xf.reshape(bsz, seq, d)
```

```python
import functools

import jax
import jax.numpy as jnp
from jax import lax
from jax.experimental import pallas as pl
from jax.experimental.pallas import tpu as pltpu

D_MODEL = 1024
DEPTH = 2
GRID_W = 64
GLA_HEADS = 4
GLA_DK = 128
GLA_DV = 256
GLA_K = GLA_HEADS * GLA_DK
GLA_V = GLA_HEADS * GLA_DV
GLA_LR = 16
GLA_TAU = 16.0
GLA_CHUNK = 64
GLA_BLOCK = 2 * GLA_CHUNK
CONV_W = 512
CONV_K = 31
CONV_HALO = 16
POOL_GROUPS = 4
POOL_W = 512
POOL_GC = 128
POOL_WINDOWS = (2, 4, 8, 16)
N_BRANCH = 3
D_FF = 4 * D_MODEL
EPS = 1e-6
LOG2E = 1.4426950408889634
LANES = 128
SUBLANES = 8
LR_PAD = LANES
PRE_CHUNK = 512
PA_Q, PA_K, PA_V, PA_POOL = 0, 1, 2, 4
MOD_ROWS = 16
VMEM_LIMIT = 56 * 1024 * 1024

F32 = jnp.float32
BF16 = jnp.bfloat16


def _dot(a, b):
    return jnp.dot(a, b, preferred_element_type=F32)


def _dot_nt(a, b):
    return lax.dot_general(a, b, (((1,), (1,)), ((), ())), preferred_element_type=F32)


def _dot_tn(a, b):
    return lax.dot_general(a, b, (((0,), (0,)), ((), ())), preferred_element_type=F32)


def _sigmoid(x):
    return jax.nn.sigmoid(x)


def _silu(x):
    return x * _sigmoid(x)


def _rms(x):
    return x * lax.rsqrt(jnp.mean(x * x, axis=-1, keepdims=True) + EPS)


def _cparams(sem):
    return pltpu.CompilerParams(dimension_semantics=sem, vmem_limit_bytes=VMEM_LIMIT)


def _full(shape):
    nd = len(shape)
    return pl.BlockSpec(shape, lambda *_: (0,) * nd)


def _ada_kernel(cc_ref, w_ref, b_ref, o_ref):
    a = _silu(cc_ref[...]).astype(BF16)
    o_ref[0] = _dot(a, w_ref[0].astype(BF16)) + b_ref[0]


def _ada(cc, w_ada, b_ada):
    tn = 512
    n = 6 * D_MODEL
    return pl.pallas_call(
        _ada_kernel,
        grid=(DEPTH, n // tn),
        in_specs=[
            pl.BlockSpec((MOD_ROWS, D_MODEL), lambda l, j: (0, 0)),
            pl.BlockSpec((1, D_MODEL, tn), lambda l, j: (l, 0, j)),
            pl.BlockSpec((1, 1, tn), lambda l, j: (l, 0, j)),
        ],
        out_specs=pl.BlockSpec((1, MOD_ROWS, tn), lambda l, j: (l, 0, j)),
        out_shape=jax.ShapeDtypeStruct((DEPTH, MOD_ROWS, n), F32),
        compiler_params=_cparams(("parallel", "parallel")),
        name="ada",
    )(cc, w_ada, b_ada.reshape(DEPTH, 1, n))


def _pre_kernel(x_ref, mod_ref, g_ref, wa_ref, wb_ref, wlr_ref, wga_ref, wgb_ref,
                wdw_ref, bdw_ref, gln_ref, bln_ref,
                pa_ref, pb_ref, lr_ref, cv_ref, *, row_len):
    tm = x_ref.shape[0]
    h = _rms(x_ref[...]) * g_ref[...]
    h = h * (1.0 + mod_ref[0, 1:2, :]) + mod_ref[0, 0:1, :]
    hb = h.astype(BF16)
    u = _dot(hb, wga_ref[...]) * _sigmoid(_dot(hb, wgb_ref[...]))

    wp = row_len + 2 * CONV_HALO
    zeros = jnp.zeros((CONV_HALO, LANES), F32)
    row_blocks = []
    for r in range(tm // row_len):
        lane_blocks = []
        for c in range(CONV_W // LANES):
            cs = slice(c * LANES, (c + 1) * LANES)
            s = jnp.concatenate([zeros, u[r * row_len:(r + 1) * row_len, cs], zeros], axis=0)
            acc = None
            for j in range(CONV_K):
                d = j - CONV_K // 2
                sh = s if d == 0 else pltpu.roll(s, (wp - d) % wp, axis=0)
                term = sh[CONV_HALO:CONV_HALO + row_len, :] * wdw_ref[j:j + 1, cs]
                acc = term if acc is None else acc + term
            lane_blocks.append(acc + bdw_ref[:, cs])
        row_blocks.append(jnp.concatenate(lane_blocks, axis=1))
    y = jnp.concatenate(row_blocks, axis=0) if len(row_blocks) > 1 else row_blocks[0]
    yc = y - jnp.mean(y, axis=-1, keepdims=True)
    yn = yc * lax.rsqrt(jnp.mean(yc * yc, axis=-1, keepdims=True) + EPS) * gln_ref[...] + bln_ref[...]
    cv_ref[...] = _silu(yn).astype(BF16)

    for i in range(wa_ref.shape[0]):
        pa_ref[i] = _dot(hb, wa_ref[i]).astype(BF16)
    for i in range(wb_ref.shape[0]):
        pb_ref[i] = _silu(_dot(hb, wb_ref[i])).astype(BF16)
    lr_ref[...] = _dot(hb, wlr_ref[...]).astype(BF16)


def _pre(xf, mod, g, w, conv, *, tm, row_len, mod_index):
    n = xf.shape[0]
    wa, wb = w[0], w[1]
    tok = lambda c: pl.BlockSpec((tm, c), lambda i: (i, 0))
    stack = lambda k: pl.BlockSpec((k, tm, PRE_CHUNK), lambda i: (0, i, 0))
    return pl.pallas_call(
        functools.partial(_pre_kernel, row_len=row_len),
        grid=(n // tm,),
        in_specs=[tok(D_MODEL),
                  pl.BlockSpec((1, 6, D_MODEL), lambda i: (mod_index(i), 0, 0)),
                  _full((1, D_MODEL))]
                 + [_full(a.shape) for a in w]
                 + [_full(a.shape) for a in conv],
        out_specs=[stack(wa.shape[0]), stack(wb.shape[0]), tok(LR_PAD), tok(CONV_W)],
        out_shape=[jax.ShapeDtypeStruct((wa.shape[0], n, PRE_CHUNK), BF16),
                   jax.ShapeDtypeStruct((wb.shape[0], n, PRE_CHUNK), BF16),
                   jax.ShapeDtypeStruct((n, LR_PAD), BF16), jax.ShapeDtypeStruct((n, CONV_W), BF16)],
        compiler_params=_cparams(("parallel",)),
        name="pre",
    )(xf, mod, g, *w, *conv)


def _gla_kernel(qf_ref, kf_ref, vf_ref, lrf_ref, qb_ref, kb_ref, vb_ref, lrb_ref, wd_ref, bd_ref,
                s0f_ref, s0b_ref, of_ref, ob_ref, sf_ref, sb_ref):
    t = qf_ref.shape[0]
    nb = t // GLA_BLOCK

    @pl.when(pl.program_id(1) == 0)
    def _():
        sf_ref[...] = s0f_ref[...]
        sb_ref[...] = s0b_ref[...]

    brow = lax.broadcasted_iota(jnp.int32, (GLA_BLOCK, GLA_BLOCK), 0)
    bcol = lax.broadcasted_iota(jnp.int32, (GLA_BLOCK, GLA_BLOCK), 1)
    masks = (bcol <= brow, bcol >= brow)
    tris = tuple(jnp.where(m, 1.0, 0.0).astype(BF16) for m in masks)
    dirs = ((qf_ref, kf_ref, vf_ref, lrf_ref, of_ref, sf_ref),
            (qb_ref, kb_ref, vb_ref, lrb_ref, ob_ref, sb_ref))

    units = [(d, blk) for blk in range(nb) for d in (0, 1)]
    units = [(d, blk if d == 0 else nb - 1 - blk) for d, blk in units]
    rows = lambda blk: slice(blk * GLA_BLOCK, (blk + 1) * GLA_BLOCK)
    kcol = lambda h: slice(h * GLA_DK, (h + 1) * GLA_DK)
    vcol = lambda h: slice(h * GLA_DV, (h + 1) * GLA_DV)
    heads = range(GLA_HEADS)

    zs = [_dot(dirs[d][3][rows(blk), :], wd_ref[:, d * GLA_K:(d + 1) * GLA_K])
          + bd_ref[:, d * GLA_K:(d + 1) * GLA_K] for d, blk in units]
    l2s = [(jnp.minimum(z, 0.0) * LOG2E - jnp.log2(1.0 + jnp.exp2(jnp.abs(z) * -LOG2E))) * (1.0 / GLA_TAU)
           for z in zs]
    his = [l2.astype(BF16) for l2 in l2s]
    los = [(l2 - hi.astype(F32)).astype(BF16) for l2, hi in zip(l2s, his)]
    tri2 = [jnp.concatenate([tr, tr], axis=1) for tr in tris]
    cbs = [_dot(tri2[d], jnp.concatenate([hi, lo], axis=0)) for (d, _), hi, lo in zip(units, his, los)]
    ops = []
    for (d, blk), cb in zip(units, cbs):
        mid = GLA_CHUNK - 1 if d == 0 else GLA_CHUNK
        end = GLA_BLOCK - 1 if d == 0 else 0
        ref = cb[mid:mid + 1, :]
        last = cb[end:end + 1, :]
        e = jnp.exp2(cb - ref)
        qs = dirs[d][0][rows(blk), :].astype(F32) * (e * GLA_DK ** -0.5)
        kr = dirs[d][1][rows(blk), :].astype(F32) * (1.0 / e)
        q2 = qs * jnp.exp2(ref)
        ke = kr * jnp.exp2(last - ref)
        gamma = jnp.exp2(last)
        ops.append((qs.astype(BF16), q2.astype(BF16), kr.astype(BF16), ke.astype(BF16), gamma))
    per_chunk = PRE_CHUNK // GLA_DV
    vs = [[dirs[d][2][h // per_chunk, rows(blk), (h % per_chunk) * GLA_DV:(h % per_chunk + 1) * GLA_DV]
           for h in heads] for d, blk in units]
    atts = [[jnp.where(masks[d], _dot_nt(op[0][:, kcol(h)], op[2][:, kcol(h)]), 0.0).astype(BF16)
             for h in heads] for (d, _), op in zip(units, ops)]
    incs = [[_dot_tn(op[3][:, kcol(h)], v[h]) for h in heads] for op, v in zip(ops, vs)]
    states = [[dirs[d][5][0, h] for h in heads] for d in (0, 1)]
    entering = []
    for (d, _), op, inc in zip(units, ops, incs):
        entering.append([st.astype(BF16) for st in states[d]])
        for h in heads:
            gcol = jnp.broadcast_to(op[4][:, kcol(h)], (GLA_DK, GLA_DK)).T
            states[d][h] = states[d][h] * jnp.concatenate([gcol] * (GLA_DV // GLA_DK), axis=1) + inc[h]
    for d in (0, 1):
        for h in heads:
            dirs[d][5][0, h] = states[d][h]
    for (d, blk), op, att, v, st in zip(units, ops, atts, vs, entering):
        for h in heads:
            o = _dot(jnp.concatenate([att[h], op[1][:, kcol(h)]], axis=1),
                     jnp.concatenate([v[h], st[h]], axis=0))
            dirs[d][4][rows(blk), vcol(h)] = o.astype(BF16)


def _gla(pa, lr, wd, bd, s0f, s0b, *, batch, t):
    n = pa.shape[1]
    nt = n // batch // t
    frow = lambda b, i: b * nt + i
    brow = lambda b, i: b * nt + nt - 1 - i
    tok = lambda c, row: pl.BlockSpec((t, c), lambda b, i: (row(b, i), 0))
    one = lambda chunk, row: pl.BlockSpec((None, t, PRE_CHUNK), lambda b, i: (chunk, row(b, i), 0))
    n_v = GLA_V // PRE_CHUNK
    val = lambda row: pl.BlockSpec((n_v, t, PRE_CHUNK), lambda b, i: (PA_V // n_v, row(b, i), 0))
    st = pl.BlockSpec((1, GLA_HEADS, GLA_DK, GLA_DV), lambda b, i: (b, 0, 0, 0))
    st_shape = jax.ShapeDtypeStruct((batch, GLA_HEADS, GLA_DK, GLA_DV), F32)
    return pl.pallas_call(
        _gla_kernel,
        grid=(batch, nt),
        in_specs=[one(PA_Q, frow), one(PA_K, frow), val(frow), tok(LR_PAD, frow),
                  one(PA_Q, brow), one(PA_K, brow), val(brow), tok(LR_PAD, brow),
                  _full(wd.shape), _full(bd.shape), st, st],
        out_specs=[tok(GLA_V, frow), tok(GLA_V, brow), st, st],
        out_shape=[jax.ShapeDtypeStruct((n, GLA_V), BF16), jax.ShapeDtypeStruct((n, GLA_V), BF16),
                   st_shape, st_shape],
        compiler_params=_cparams(("parallel", "arbitrary")),
        name="gla",
    )(pa, pa, pa, lr, pa, pa, pa, lr, wd, bd, s0f, s0b)


def _window_counts(n, left, right, shape, axis):
    t = lax.broadcasted_iota(jnp.int32, shape, axis)
    lo = jnp.clip(t - left, 0, n)
    hi = jnp.clip(t + right + 1, 0, n)
    return (hi - lo).astype(F32)


def _mix_kernel(x_ref, mod_ref, gpre_ref, gpost_ref, of_ref, ob_ref, sg_ref, cv_ref, up_ref,
                wgt_ref, bgt_ref, ggla_ref, wgo_ref, wco_ref, wpg_ref, sp_ref, wpo_ref, wout_ref,
                o_ref, pd_scr, *, lead):
    tm = pd_scr.shape[0] if not lead else x_ref.shape[0] * x_ref.shape[1]
    ld = lambda ref: ref[...].reshape(tm, ref.shape[-1])

    @pl.when(pl.program_id(2) == 0)
    def _():
        up = up_ref[...].astype(F32)
        n = up.shape[0]
        for g, w in enumerate(POOL_WINDOWS):
            left, right = w // 2, w - 1 - w // 2
            ug = up[..., g * POOL_GC:(g + 1) * POOL_GC]
            total = ug
            for d in range(-left, right + 1):
                if d == 0:
                    continue
                if lead:
                    pad = jnp.zeros((abs(d),) + ug.shape[1:], F32)
                    sh = (jnp.concatenate([ug[d:], pad], axis=0) if d > 0
                          else jnp.concatenate([pad, ug[:n + d]], axis=0))
                else:
                    tpos = lax.broadcasted_iota(jnp.int32, ug.shape, 0)
                    sh = jnp.where((tpos + d >= 0) & (tpos + d < n),
                                   pltpu.roll(ug, (n - d) % n, axis=0), 0.0)
                total = total + sh
            cnt = _window_counts(n, left, right, (n,) + (1,) * (ug.ndim - 1), 0)
            diff = (total / cnt - ug).reshape(-1, POOL_GC).astype(BF16)
            cs = slice(g * POOL_GC, (g + 1) * POOL_GC)
            pd_scr[:, cs] = (_dot(diff, wpg_ref[g]) * sp_ref[:, cs]).astype(BF16)

    parts = 2 if lead else 1
    tp = tm // parts
    lead_rows = x_ref.shape[0] // parts

    def part(ref, p, *idx):
        r = ref.at[idx] if idx else ref
        v = r[p * lead_rows:(p + 1) * lead_rows] if lead else r[p * tp:(p + 1) * tp, :]
        return v.reshape(tp, v.shape[-1])

    groups = range(parts)
    xs = [part(x_ref, p) for p in groups]
    hbs = [((_rms(x) * gpre_ref[...]) * (1.0 + mod_ref[0, 1:2, :]) + mod_ref[0, 0:1, :]).astype(BF16) for x in xs]

    def gla_in(p):
        o = part(of_ref, p).astype(F32) + part(ob_ref, p).astype(F32)
        on = jnp.concatenate([_rms(o[:, hh * GLA_DV:(hh + 1) * GLA_DV]) for hh in range(GLA_HEADS)], axis=-1)
        sg = jnp.concatenate([part(sg_ref, p, i) for i in range(sg_ref.shape[0])], axis=1)
        return (on * ggla_ref[...] * sg.astype(F32)).astype(BF16)

    a_ins = [gla_in(p) for p in groups]
    gate_pre = [[_dot(hb, wgt_ref[:, i * D_MODEL:(i + 1) * D_MODEL]) + bgt_ref[:, i * D_MODEL:(i + 1) * D_MODEL]
                 for i in range(N_BRANCH)] for hb in hbs]
    yas = [_dot(a, wgo_ref[...]) for a in a_ins]
    ybs = [_dot(part(cv_ref, p), wco_ref[...]) for p in groups]
    if lead:
        base = pl.multiple_of(pl.program_id(2) * tm, tm)
        pds = [pd_scr[pl.ds(base + p * tp, tp), :] for p in groups]
    else:
        pds = [pd_scr[...]]
    ycs = [_dot(pd, wpo_ref[...]) for pd in pds]
    mixed = [(_sigmoid(g[0]) * ya + _sigmoid(g[1]) * yb + _sigmoid(g[2]) * yc).astype(BF16)
             for g, ya, yb, yc in zip(gate_pre, yas, ybs, ycs)]
    ys = [_dot(m, wout_ref[...]) for m in mixed]
    for p, x, y in zip(groups, xs, ys):
        out = x + mod_ref[0, 2:3, :] * (_rms(y) * gpost_ref[...])
        if lead:
            o_ref[p * lead_rows:(p + 1) * lead_rows] = out.reshape((lead_rows,) + o_ref.shape[1:])
        else:
            o_ref[...] = out


def _mix_weight_specs(ws):
    return [_full(a.shape) for a in ws]


def _mix_latent(x4, mod, gpre, gpost, o_f, o_b, sg, cv, up, ws, *, rb, cb):
    bsz, rows, cols, _ = x4.shape
    blk = lambda c: pl.BlockSpec((None, rb, cb, c), lambda b, ci, ri: (b, ri, ci, 0))
    return pl.pallas_call(
        functools.partial(_mix_kernel, lead=True),
        grid=(bsz, cols // cb, rows // rb),
        in_specs=[blk(D_MODEL),
                  pl.BlockSpec((1, 6, D_MODEL), lambda b, ci, ri: (b, 0, 0)),
                  _full((1, D_MODEL)), _full((1, D_MODEL)),
                  blk(GLA_V), blk(GLA_V),
                  pl.BlockSpec((sg.shape[0], None, rb, cb, PRE_CHUNK), lambda b, ci, ri: (0, b, ri, ci, 0)),
                  blk(CONV_W),
                  pl.BlockSpec((None, None, rows, cb, POOL_W), lambda b, ci, ri: (PA_POOL, b, 0, ci, 0))]
                 + _mix_weight_specs(ws),
        out_specs=blk(D_MODEL),
        out_shape=jax.ShapeDtypeStruct(x4.shape, F32),
        scratch_shapes=[pltpu.VMEM((rows * cb, POOL_W), BF16)],
        compiler_params=_cparams(("parallel", "parallel", "arbitrary")),
        name="mix",
    )(x4, mod, gpre, gpost, o_f, o_b, sg, cv, up, *ws)


def _mix_ctx(xf, mod, gpre, gpost, o_f, o_b, sg, cv, up, ws, *, batch):
    n = xf.shape[0]
    tm = n // batch
    blk = lambda c: pl.BlockSpec((tm, c), lambda b, ci, ri: (b, 0))
    return pl.pallas_call(
        functools.partial(_mix_kernel, lead=False),
        grid=(batch, 1, 1),
        in_specs=[blk(D_MODEL),
                  pl.BlockSpec((1, 6, D_MODEL), lambda b, ci, ri: (batch, 0, 0)),
                  _full((1, D_MODEL)), _full((1, D_MODEL)),
                  blk(GLA_V), blk(GLA_V),
                  pl.BlockSpec((sg.shape[0], tm, PRE_CHUNK), lambda b, ci, ri: (0, b, 0)),
                  blk(CONV_W),
                  pl.BlockSpec((None, tm, POOL_W), lambda b, ci, ri: (PA_POOL, b, 0))]
                 + _mix_weight_specs(ws),
        out_specs=blk(D_MODEL),
        out_shape=jax.ShapeDtypeStruct(xf.shape, F32),
        scratch_shapes=[pltpu.VMEM((tm, POOL_W), BF16)],
        compiler_params=_cparams(("parallel", "parallel", "arbitrary")),
        name="mix_ctx",
    )(xf, mod, gpre, gpost, o_f, o_b, sg, cv, up, *ws)


def _mlp_kernel(x_ref, mod_ref, gpre_ref, gpost_ref, w1_ref, w2_ref, o_ref):
    x = x_ref[...]
    h = _rms(x) * gpre_ref[...]
    hb = (h * (1.0 + mod_ref[0, 4:5, :]) + mod_ref[0, 3:4, :]).astype(BF16)
    y = jnp.zeros(x.shape, F32)
    for c in range(D_FF // D_MODEL):
        cs = slice(c * D_MODEL, (c + 1) * D_MODEL)
        a = jnp.maximum(_dot(hb, w1_ref[:, cs]), 0.0)
        y = y + _dot((a * a).astype(BF16), w2_ref[cs, :])
    o_ref[...] = x + mod_ref[0, 5:6, :] * (_rms(y) * gpost_ref[...])


def _mlp(xf, mod, gpre, gpost, w1, w2, *, tm, mod_index):
    n = xf.shape[0]
    tok = pl.BlockSpec((tm, D_MODEL), lambda i: (i, 0))
    return pl.pallas_call(
        _mlp_kernel,
        grid=(n // tm,),
        in_specs=[tok, pl.BlockSpec((1, 6, D_MODEL), lambda i: (mod_index(i), 0, 0)),
                  _full((1, D_MODEL)), _full((1, D_MODEL)), _full(w1.shape), _full(w2.shape)],
        out_specs=tok,
        out_shape=jax.ShapeDtypeStruct(xf.shape, F32),
        compiler_params=_cparams(("parallel",)),
        name="mlp",
    )(xf, mod, gpre, gpost, w1, w2)


def kernel(x, c, ctx, c_ctx, w_ada, b_ada, g_pre_mix, g_post_mix, g_pre_mlp, g_post_mlp, w_in, w_decay,
           b_decay, g_gla, w_gla_o, w_dw, b_dw, g_conv_ln, b_conv_ln, w_conv_o, w_pool_g, s_pool,
           w_pool_o, b_gate, w_out, w_mlp1, w_mlp2):
    bsz, seq, d = x.shape
    ctx_len = ctx.shape[1]
    rows = seq // GRID_W
    n_lat, n_ctx = bsz * seq, bsz * ctx_len
    tm_pre, tm_mlp = 512, 1024
    row = lambda a: a.reshape(1, -1)

    cc = jnp.zeros((MOD_ROWS, d), F32).at[:bsz].set(c).at[bsz].set(c_ctx)
    mods = _ada(cc, w_ada, b_ada).reshape(DEPTH, MOD_ROWS, 6, d)

    xf = x.reshape(n_lat, d)
    cf = ctx.reshape(n_ctx, d)
    lat_mod = lambda t: (lambda i: i // (seq // t))
    ctx_mod = lambda i: bsz
    zero_state = jnp.zeros((bsz, GLA_HEADS, GLA_DK, GLA_DV), F32)

    for l in range(DEPTH):
        last = l == DEPTH - 1
        mod = mods[l]
        edges = [0]
        for s in (GLA_K, GLA_K, GLA_V, GLA_V, GLA_LR, GLA_LR, CONV_W, CONV_W, POOL_W, N_BRANCH * d):
            edges.append(edges[-1] + s)
        seg = lambda i, j=None: w_in[l, :, edges[i]:edges[i + 1 if j is None else j]].astype(BF16)
        w_lr = jnp.pad(seg(4, 6), ((0, 0), (0, LR_PAD - 2 * GLA_LR)))
        chunks = lambda a: [a[:, c:c + PRE_CHUNK] for c in range(0, a.shape[1], PRE_CHUNK)]
        w_plain = jnp.stack(chunks(seg(0)) + chunks(seg(1)) + chunks(seg(2)) + chunks(seg(8)))
        w_gated = jnp.stack(chunks(seg(3)))
        pre_w = (w_plain, w_gated, w_lr, seg(6), seg(7))
        conv_p = (w_dw[l], row(b_dw[l]), row(g_conv_ln[l]), row(b_conv_ln[l]))
        wd = jnp.zeros((LR_PAD, 2 * GLA_K), F32)
        wd = wd.at[:GLA_LR, :GLA_K].set(w_decay[l, 0]).at[GLA_LR:2 * GLA_LR, GLA_K:].set(w_decay[l, 1])
        wd = wd.astype(BF16)
        bd = b_decay[l].reshape(1, 2 * GLA_K)
        mix_w = (seg(9), b_gate[l].reshape(1, N_BRANCH * d), row(g_gla[l]), w_gla_o[l].astype(BF16),
                 w_conv_o[l].astype(BF16), w_pool_g[l].astype(BF16), row(s_pool[l]),
                 w_pool_o[l].astype(BF16), w_out[l].astype(BF16))
        w1, w2 = w_mlp1[l].astype(BF16), w_mlp2[l].astype(BF16)
        gpm, gqm = row(g_pre_mix[l]), row(g_post_mix[l])
        gpf, gqf = row(g_pre_mlp[l]), row(g_post_mlp[l])

        pa, pb, lr, cv = _pre(xf, mod, gpm, pre_w, conv_p, tm=tm_pre, row_len=GRID_W, mod_index=lat_mod(tm_pre))
        cpa, cpb, clr, ccv = _pre(cf, mod, gpm, pre_w, conv_p, tm=ctx_len, row_len=ctx_len, mod_index=ctx_mod)
        co_f, co_b, cs_f, cs_b = _gla(cpa, clr, wd, bd, zero_state, zero_state, batch=bsz, t=ctx_len)
        o_f, o_b, _, _ = _gla(pa, lr, wd, bd, cs_f, cs_b, batch=bsz, t=512)

        r4 = lambda a: a.reshape(a.shape[:-2] + (bsz, rows, GRID_W, a.shape[-1]))
        x1 = _mix_latent(r4(xf), mod, gpm, gqm, r4(o_f), r4(o_b), r4(pb), r4(cv), r4(pa), mix_w, rb=32, cb=16)
        xf = _mlp(x1.reshape(n_lat, d), mod, gpf, gqf, w1, w2, tm=tm_mlp, mod_index=lat_mod(tm_mlp))
        if not last:
            c1 = _mix_ctx(cf, mod, gpm, gqm, co_f, co_b, cpb, ccv, cpa, mix_w, batch=bsz)
            cf = _mlp(c1, mod, gpf, gqf, w1, w2, tm=min(tm_mlp, n_ctx), mod_index=ctx_mod)
    return xf.reshape(bsz, seq, d)
```

```python
import functools

import jax
import jax.numpy as jnp
from jax import lax
from jax.experimental import pallas as pl
from jax.experimental.pallas import tpu as pltpu

D_MODEL = 1024
DEPTH = 2
GRID_W = 64
GLA_HEADS = 4
GLA_DK = 128
GLA_DV = 256
GLA_K = GLA_HEADS * GLA_DK
GLA_V = GLA_HEADS * GLA_DV
GLA_LR = 16
GLA_TAU = 16.0
GLA_CHUNK = 64
GLA_BLOCK = 2 * GLA_CHUNK
CONV_W = 512
CONV_K = 31
CONV_HALO = 16
POOL_GROUPS = 4
POOL_W = 512
POOL_GC = 128
POOL_WINDOWS = (2, 4, 8, 16)
N_BRANCH = 3
D_FF = 4 * D_MODEL
EPS = 1e-6
LOG2E = 1.4426950408889634
LANES = 128
LR_PAD = LANES
PRE_CHUNK = 512
MLP_CHUNK = 2048
PA_Q, PA_K, PA_V, PA_POOL = 0, 1, 2, 4
MOD_ROWS = 16
VMEM_LIMIT = 56 * 1024 * 1024

F32 = jnp.float32
BF16 = jnp.bfloat16


def _dot(a, b):
    return jnp.dot(a, b, preferred_element_type=F32)


def _dot_nt(a, b):
    return lax.dot_general(a, b, (((1,), (1,)), ((), ())), preferred_element_type=F32)


def _dot_tn(a, b):
    return lax.dot_general(a, b, (((0,), (0,)), ((), ())), preferred_element_type=F32)


def _sigmoid(x):
    return jax.nn.sigmoid(x)


def _silu(x):
    return x * _sigmoid(x)


def _rms(x):
    return x * lax.rsqrt(jnp.mean(x * x, axis=-1, keepdims=True) + EPS)


def _cparams(sem):
    return pltpu.CompilerParams(dimension_semantics=sem, vmem_limit_bytes=VMEM_LIMIT)


def _full(shape):
    nd = len(shape)
    return pl.BlockSpec(shape, lambda *_: (0,) * nd)


def _ada_kernel(cc_ref, w_ref, b_ref, o_ref):
    a = _silu(cc_ref[...]).astype(BF16)
    o_ref[0] = _dot(a, w_ref[0].astype(BF16)) + b_ref[0]


def _ada(cc, w_ada, b_ada):
    tn = 1024
    n = 6 * D_MODEL
    return pl.pallas_call(
        _ada_kernel,
        grid=(DEPTH, n // tn),
        in_specs=[
            pl.BlockSpec((MOD_ROWS, D_MODEL), lambda l, j: (0, 0)),
            pl.BlockSpec((1, D_MODEL, tn), lambda l, j: (l, 0, j)),
            pl.BlockSpec((1, 1, tn), lambda l, j: (l, 0, j)),
        ],
        out_specs=pl.BlockSpec((1, MOD_ROWS, tn), lambda l, j: (l, 0, j)),
        out_shape=jax.ShapeDtypeStruct((DEPTH, MOD_ROWS, n), F32),
        compiler_params=_cparams(("parallel", "parallel")),
        name="ada",
    )(cc, w_ada, b_ada.reshape(DEPTH, 1, n))


def _pre_kernel(x_ref, mod_ref, g_ref, wa_ref, wb_ref, wlr_ref, wga_ref, wgb_ref,
                wdw_ref, bdw_ref, gln_ref, bln_ref,
                pa_ref, pb_ref, lr_ref, cv_ref, *, row_len):
    tm = x_ref.shape[0]
    h = _rms(x_ref[...]) * g_ref[...]
    h = h * (1.0 + mod_ref[0, 1:2, :]) + mod_ref[0, 0:1, :]
    hb = h.astype(BF16)
    u = _dot(hb, wga_ref[...]) * _sigmoid(_dot(hb, wgb_ref[...]))

    wp = row_len + 2 * CONV_HALO
    zeros = jnp.zeros((CONV_HALO, LANES), F32)
    row_blocks = []
    for r in range(tm // row_len):
        lane_blocks = []
        for c in range(CONV_W // LANES):
            cs = slice(c * LANES, (c + 1) * LANES)
            s = jnp.concatenate([zeros, u[r * row_len:(r + 1) * row_len, cs], zeros], axis=0)
            acc = None
            for j in range(CONV_K):
                d = j - CONV_K // 2
                sh = s if d == 0 else pltpu.roll(s, (wp - d) % wp, axis=0)
                term = sh[CONV_HALO:CONV_HALO + row_len, :] * wdw_ref[j:j + 1, cs]
                acc = term if acc is None else acc + term
            lane_blocks.append(acc + bdw_ref[:, cs])
        row_blocks.append(jnp.concatenate(lane_blocks, axis=1))
    y = jnp.concatenate(row_blocks, axis=0) if len(row_blocks) > 1 else row_blocks[0]
    yc = y - jnp.mean(y, axis=-1, keepdims=True)
    yn = yc * lax.rsqrt(jnp.mean(yc * yc, axis=-1, keepdims=True) + EPS) * gln_ref[...] + bln_ref[...]
    cv_ref[...] = _silu(yn).astype(BF16)

    for i in range(wa_ref.shape[0]):
        pa_ref[i] = _dot(hb, wa_ref[i]).astype(BF16)
    for i in range(wb_ref.shape[0]):
        pb_ref[i] = _silu(_dot(hb, wb_ref[i])).astype(BF16)
    lr_ref[...] = _dot(hb, wlr_ref[...]).astype(BF16)


def _pre(xf, mod, g, w, conv, *, tm, row_len, mod_index):
    n = xf.shape[0]
    wa, wb = w[0], w[1]
    tok = lambda c: pl.BlockSpec((tm, c), lambda i: (i, 0))
    stack = lambda k: pl.BlockSpec((k, tm, PRE_CHUNK), lambda i: (0, i, 0))
    return pl.pallas_call(
        functools.partial(_pre_kernel, row_len=row_len),
        grid=(n // tm,),
        in_specs=[tok(D_MODEL),
                  pl.BlockSpec((1, 6, D_MODEL), lambda i: (mod_index(i), 0, 0)),
                  _full((1, D_MODEL))]
                 + [_full(a.shape) for a in w]
                 + [_full(a.shape) for a in conv],
        out_specs=[stack(wa.shape[0]), stack(wb.shape[0]), tok(LR_PAD), tok(CONV_W)],
        out_shape=[jax.ShapeDtypeStruct((wa.shape[0], n, PRE_CHUNK), BF16),
                   jax.ShapeDtypeStruct((wb.shape[0], n, PRE_CHUNK), BF16),
                   jax.ShapeDtypeStruct((n, LR_PAD), BF16), jax.ShapeDtypeStruct((n, CONV_W), BF16)],
        compiler_params=_cparams(("parallel",)),
        name="pre",
    )(xf, mod, g, *w, *conv)


def _gla_kernel(qf_ref, kf_ref, vf_ref, lrf_ref, qb_ref, kb_ref, vb_ref, lrb_ref, wd_ref, bd_ref,
                s0f_ref, s0b_ref, of_ref, ob_ref, sf_ref, sb_ref):
    t = qf_ref.shape[0]
    nb = t // GLA_BLOCK

    @pl.when(pl.program_id(1) == 0)
    def _():
        sf_ref[...] = s0f_ref[...]
        sb_ref[...] = s0b_ref[...]

    brow = lax.broadcasted_iota(jnp.int32, (GLA_BLOCK, GLA_BLOCK), 0)
    bcol = lax.broadcasted_iota(jnp.int32, (GLA_BLOCK, GLA_BLOCK), 1)
    masks = (bcol <= brow, bcol >= brow)
    tris = tuple(jnp.where(m, 1.0, 0.0).astype(BF16) for m in masks)
    dirs = ((qf_ref, kf_ref, vf_ref, lrf_ref, of_ref, sf_ref),
            (qb_ref, kb_ref, vb_ref, lrb_ref, ob_ref, sb_ref))

    units = [(d, blk) for blk in range(nb) for d in (0, 1)]
    units = [(d, blk if d == 0 else nb - 1 - blk) for d, blk in units]
    rows = lambda blk: slice(blk * GLA_BLOCK, (blk + 1) * GLA_BLOCK)
    kcol = lambda h: slice(h * GLA_DK, (h + 1) * GLA_DK)
    vcol = lambda h: slice(h * GLA_DV, (h + 1) * GLA_DV)
    heads = range(GLA_HEADS)

    tri2 = [jnp.concatenate([tr, tr], axis=1) for tr in tris]
    per_chunk = PRE_CHUNK // GLA_DV

    def prep(us):
        zs = [_dot(dirs[d][3][rows(blk), :], wd_ref[:, d * GLA_K:(d + 1) * GLA_K])
              + bd_ref[:, d * GLA_K:(d + 1) * GLA_K] for d, blk in us]
        l2s = [(jnp.minimum(z, 0.0) * LOG2E - jnp.log2(1.0 + jnp.exp2(jnp.abs(z) * -LOG2E))) * (1.0 / GLA_TAU)
               for z in zs]
        his = [l2.astype(BF16) for l2 in l2s]
        los = [(l2 - hi.astype(F32)).astype(BF16) for l2, hi in zip(l2s, his)]
        cbs = [_dot(tri2[d], jnp.concatenate([hi, lo], axis=0)) for (d, _), hi, lo in zip(us, his, los)]
        ops = []
        for (d, blk), cb in zip(us, cbs):
            mid = GLA_CHUNK - 1 if d == 0 else GLA_CHUNK
            end = GLA_BLOCK - 1 if d == 0 else 0
            ref = cb[mid:mid + 1, :]
            last = cb[end:end + 1, :]
            e = jnp.exp2(cb - ref)
            qs = dirs[d][0][rows(blk), :].astype(F32) * (e * GLA_DK ** -0.5)
            kr = dirs[d][1][rows(blk), :].astype(F32) * (1.0 / e)
            q2 = qs * jnp.exp2(ref)
            ke = kr * jnp.exp2(last - ref)
            gamma = jnp.exp2(last)
            ops.append((qs.astype(BF16), q2.astype(BF16), kr.astype(BF16), ke.astype(BF16), gamma))
        return ops

    def score(us, ops):
        vs = [[dirs[d][2][h // per_chunk, rows(blk), (h % per_chunk) * GLA_DV:(h % per_chunk + 1) * GLA_DV]
               for h in heads] for d, blk in us]
        atts = [[jnp.where(masks[d], _dot_nt(op[0][:, kcol(h)], op[2][:, kcol(h)]), 0.0).astype(BF16)
                 for h in heads] for (d, _), op in zip(us, ops)]
        incs = [[_dot_tn(op[3][:, kcol(h)], v[h]) for h in heads] for op, v in zip(ops, vs)]
        return vs, atts, incs

    states = [[dirs[d][5][0, h] for h in heads] for d in (0, 1)]

    def carry(us, ops, incs):
        entering = []
        for (d, _), op, inc in zip(us, ops, incs):
            entering.append([st.astype(BF16) for st in states[d]])
            for h in heads:
                gcol = jnp.broadcast_to(op[4][:, kcol(h)], (GLA_DK, GLA_DK)).T
                states[d][h] = states[d][h] * jnp.concatenate([gcol] * (GLA_DV // GLA_DK), axis=1) + inc[h]
        return entering

    def emit(us, ops, atts, vs, entering):
        for (d, blk), op, att, v, st in zip(us, ops, atts, vs, entering):
            for h in heads:
                o = _dot(jnp.concatenate([att[h], op[1][:, kcol(h)]], axis=1),
                         jnp.concatenate([v[h], st[h]], axis=0))
                dirs[d][4][rows(blk), vcol(h)] = o.astype(BF16)

    ops = prep(units)
    vs, atts, incs = score(units, ops)
    entering = carry(units, ops, incs)
    emit(units, ops, atts, vs, entering)
    for d in (0, 1):
        for h in heads:
            dirs[d][5][0, h] = states[d][h]


def _gla(pa, lr, wd, bd, s0f, s0b, *, batch, t):
    n = pa.shape[1]
    nt = n // batch // t
    frow = lambda b, i: b * nt + i
    brow = lambda b, i: b * nt + nt - 1 - i
    tok = lambda c, row: pl.BlockSpec((t, c), lambda b, i: (row(b, i), 0))
    one = lambda chunk, row: pl.BlockSpec((None, t, PRE_CHUNK), lambda b, i: (chunk, row(b, i), 0))
    n_v = GLA_V // PRE_CHUNK
    val = lambda row: pl.BlockSpec((n_v, t, PRE_CHUNK), lambda b, i: (PA_V // n_v, row(b, i), 0))
    st = pl.BlockSpec((1, GLA_HEADS, GLA_DK, GLA_DV), lambda b, i: (b, 0, 0, 0))
    st_shape = jax.ShapeDtypeStruct((batch, GLA_HEADS, GLA_DK, GLA_DV), F32)
    return pl.pallas_call(
        _gla_kernel,
        grid=(batch, nt),
        in_specs=[one(PA_Q, frow), one(PA_K, frow), val(frow), tok(LR_PAD, frow),
                  one(PA_Q, brow), one(PA_K, brow), val(brow), tok(LR_PAD, brow),
                  _full(wd.shape), _full(bd.shape), st, st],
        out_specs=[tok(GLA_V, frow), tok(GLA_V, brow), st, st],
        out_shape=[jax.ShapeDtypeStruct((n, GLA_V), BF16), jax.ShapeDtypeStruct((n, GLA_V), BF16),
                   st_shape, st_shape],
        compiler_params=_cparams(("parallel", "arbitrary")),
        name="gla",
    )(pa, pa, pa, lr, pa, pa, pa, lr, wd, bd, s0f, s0b)


def _window_counts(n, left, right, shape, axis):
    t = lax.broadcasted_iota(jnp.int32, shape, axis)
    lo = jnp.clip(t - left, 0, n)
    hi = jnp.clip(t + right + 1, 0, n)
    return (hi - lo).astype(F32)


def _mix_kernel(x_ref, mod_ref, gpre_ref, gpost_ref, of_ref, ob_ref, sg_ref, cv_ref, up_ref,
                wgt_ref, bgt_ref, ggla_ref, wgo_ref, wco_ref, wpg_ref, sp_ref, wpo_ref, wout_ref,
                o_ref, pd_scr, *, lead):
    tm = pd_scr.shape[0] if not lead else x_ref.shape[0] * x_ref.shape[1]

    @pl.when(pl.program_id(2) == 0)
    def _():
        up = up_ref[...].astype(F32)
        n = up.shape[0]
        for g, w in enumerate(POOL_WINDOWS):
            left, right = w // 2, w - 1 - w // 2
            ug = up[..., g * POOL_GC:(g + 1) * POOL_GC]
            total = ug
            for d in range(-left, right + 1):
                if d == 0:
                    continue
                if lead:
                    pad = jnp.zeros((abs(d),) + ug.shape[1:], F32)
                    sh = (jnp.concatenate([ug[d:], pad], axis=0) if d > 0
                          else jnp.concatenate([pad, ug[:n + d]], axis=0))
                else:
                    tpos = lax.broadcasted_iota(jnp.int32, ug.shape, 0)
                    sh = jnp.where((tpos + d >= 0) & (tpos + d < n),
                                   pltpu.roll(ug, (n - d) % n, axis=0), 0.0)
                total = total + sh
            cnt = _window_counts(n, left, right, (n,) + (1,) * (ug.ndim - 1), 0)
            diff = (total / cnt - ug).reshape(-1, POOL_GC).astype(BF16)
            cs = slice(g * POOL_GC, (g + 1) * POOL_GC)
            pd_scr[:, cs] = (_dot(diff, wpg_ref[g]) * sp_ref[:, cs]).astype(BF16)

    parts = 2 if lead else 1
    tp = tm // parts
    lead_rows = x_ref.shape[0] // parts

    def part(ref, p, *idx):
        r = ref.at[idx] if idx else ref
        v = r[p * lead_rows:(p + 1) * lead_rows] if lead else r[p * tp:(p + 1) * tp, :]
        return v.reshape(tp, v.shape[-1])

    groups = range(parts)
    xs = [part(x_ref, p) for p in groups]
    hbs = [((_rms(x) * gpre_ref[...]) * (1.0 + mod_ref[0, 1:2, :]) + mod_ref[0, 0:1, :]).astype(BF16) for x in xs]

    def gla_in(p):
        o = part(of_ref, p).astype(F32) + part(ob_ref, p).astype(F32)
        on = jnp.concatenate([_rms(o[:, hh * GLA_DV:(hh + 1) * GLA_DV]) for hh in range(GLA_HEADS)], axis=-1)
        sg = jnp.concatenate([part(sg_ref, p, i) for i in range(sg_ref.shape[0])], axis=1)
        return (on * ggla_ref[...] * sg.astype(F32)).astype(BF16)

    a_ins = [gla_in(p) for p in groups]
    gate_pre = [[_dot(hb, wgt_ref[:, i * D_MODEL:(i + 1) * D_MODEL]) + bgt_ref[:, i * D_MODEL:(i + 1) * D_MODEL]
                 for i in range(N_BRANCH)] for hb in hbs]
    yas = [_dot(a, wgo_ref[...]) for a in a_ins]
    ybs = [_dot(part(cv_ref, p), wco_ref[...]) for p in groups]
    if lead:
        base = pl.multiple_of(pl.program_id(2) * tm, tm)
        pds = [pd_scr[pl.ds(base + p * tp, tp), :] for p in groups]
    else:
        pds = [pd_scr[...]]
    ycs = [_dot(pd, wpo_ref[...]) for pd in pds]
    mixed = [(_sigmoid(g[0]) * ya + _sigmoid(g[1]) * yb + _sigmoid(g[2]) * yc).astype(BF16)
             for g, ya, yb, yc in zip(gate_pre, yas, ybs, ycs)]
    ys = [_dot(m, wout_ref[...]) for m in mixed]
    for p, x, y in zip(groups, xs, ys):
        out = x + mod_ref[0, 2:3, :] * (_rms(y) * gpost_ref[...])
        if lead:
            o_ref[p * lead_rows:(p + 1) * lead_rows] = out.reshape((lead_rows,) + o_ref.shape[1:])
        else:
            o_ref[...] = out


def _mix_weight_specs(ws):
    return [_full(a.shape) for a in ws]


def _mix_latent(x4, mod, gpre, gpost, o_f, o_b, sg, cv, up, ws, *, rb, cb):
    bsz, rows, cols, _ = x4.shape
    blk = lambda c: pl.BlockSpec((None, rb, cb, c), lambda b, ci, ri: (b, ri, ci, 0))
    return pl.pallas_call(
        functools.partial(_mix_kernel, lead=True),
        grid=(bsz, cols // cb, rows // rb),
        in_specs=[blk(D_MODEL),
                  pl.BlockSpec((1, 6, D_MODEL), lambda b, ci, ri: (b, 0, 0)),
                  _full((1, D_MODEL)), _full((1, D_MODEL)),
                  blk(GLA_V), blk(GLA_V),
                  pl.BlockSpec((sg.shape[0], None, rb, cb, PRE_CHUNK), lambda b, ci, ri: (0, b, ri, ci, 0)),
                  blk(CONV_W),
                  pl.BlockSpec((None, None, rows, cb, POOL_W), lambda b, ci, ri: (PA_POOL, b, 0, ci, 0))]
                 + _mix_weight_specs(ws),
        out_specs=blk(D_MODEL),
        out_shape=jax.ShapeDtypeStruct(x4.shape, F32),
        scratch_shapes=[pltpu.VMEM((rows * cb, POOL_W), BF16)],
        compiler_params=_cparams(("parallel", "parallel", "arbitrary")),
        name="mix",
    )(x4, mod, gpre, gpost, o_f, o_b, sg, cv, up, *ws)


def _mix_ctx(xf, mod, gpre, gpost, o_f, o_b, sg, cv, up, ws, *, batch):
    n = xf.shape[0]
    tm = n // batch
    blk = lambda c: pl.BlockSpec((tm, c), lambda b, ci, ri: (b, 0))
    return pl.pallas_call(
        functools.partial(_mix_kernel, lead=False),
        grid=(batch, 1, 1),
        in_specs=[blk(D_MODEL),
                  pl.BlockSpec((1, 6, D_MODEL), lambda b, ci, ri: (batch, 0, 0)),
                  _full((1, D_MODEL)), _full((1, D_MODEL)),
                  blk(GLA_V), blk(GLA_V),
                  pl.BlockSpec((sg.shape[0], tm, PRE_CHUNK), lambda b, ci, ri: (0, b, 0)),
                  blk(CONV_W),
                  pl.BlockSpec((None, tm, POOL_W), lambda b, ci, ri: (PA_POOL, b, 0))]
                 + _mix_weight_specs(ws),
        out_specs=blk(D_MODEL),
        out_shape=jax.ShapeDtypeStruct(xf.shape, F32),
        scratch_shapes=[pltpu.VMEM((tm, POOL_W), BF16)],
        compiler_params=_cparams(("parallel", "parallel", "arbitrary")),
        name="mix_ctx",
    )(xf, mod, gpre, gpost, o_f, o_b, sg, cv, up, *ws)


def _mlp_kernel(x_ref, mod_ref, gpre_ref, gpost_ref, w1_ref, w2_ref, o_ref):
    tm = x_ref.shape[0]
    parts = 2 if tm >= 1024 else 1
    tp = tm // parts
    rows = lambda p: slice(p * tp, (p + 1) * tp)
    xs = [x_ref[rows(p), :] for p in range(parts)]
    hbs = [((_rms(x) * gpre_ref[...]) * (1.0 + mod_ref[0, 4:5, :]) + mod_ref[0, 3:4, :]).astype(BF16) for x in xs]
    ys = [None] * parts
    for c in range(D_FF // MLP_CHUNK):
        cs = slice(c * MLP_CHUNK, (c + 1) * MLP_CHUNK)
        acts = [jnp.maximum(_dot(hb, w1_ref[:, cs]), 0.0) for hb in hbs]
        for p, a in enumerate(acts):
            t = _dot((a * a).astype(BF16), w2_ref[cs, :])
            ys[p] = t if ys[p] is None else ys[p] + t
    for p, (x, y) in enumerate(zip(xs, ys)):
        o_ref[rows(p), :] = x + mod_ref[0, 5:6, :] * (_rms(y) * gpost_ref[...])


def _mlp(xf, mod, gpre, gpost, w1, w2, *, tm, mod_index):
    n = xf.shape[0]
    tok = pl.BlockSpec((tm, D_MODEL), lambda i: (i, 0))
    return pl.pallas_call(
        _mlp_kernel,
        grid=(n // tm,),
        in_specs=[tok, pl.BlockSpec((1, 6, D_MODEL), lambda i: (mod_index(i), 0, 0)),
                  _full((1, D_MODEL)), _full((1, D_MODEL)), _full(w1.shape), _full(w2.shape)],
        out_specs=tok,
        out_shape=jax.ShapeDtypeStruct(xf.shape, F32),
        compiler_params=_cparams(("parallel",)),
        name="mlp",
    )(xf, mod, gpre, gpost, w1, w2)


def kernel(x, c, ctx, c_ctx, w_ada, b_ada, g_pre_mix, g_post_mix, g_pre_mlp, g_post_mlp, w_in, w_decay,
           b_decay, g_gla, w_gla_o, w_dw, b_dw, g_conv_ln, b_conv_ln, w_conv_o, w_pool_g, s_pool,
           w_pool_o, b_gate, w_out, w_mlp1, w_mlp2):
    bsz, seq, d = x.shape
    ctx_len = ctx.shape[1]
    rows = seq // GRID_W
    n_lat, n_ctx = bsz * seq, bsz * ctx_len
    tm_pre, tm_mlp = 1024, 1024
    row = lambda a: a.reshape(1, -1)

    cc = jnp.zeros((MOD_ROWS, d), F32).at[:bsz].set(c).at[bsz].set(c_ctx)
    mods = _ada(cc, w_ada, b_ada).reshape(DEPTH, MOD_ROWS, 6, d)

    xf = x.reshape(n_lat, d)
    cf = ctx.reshape(n_ctx, d)
    lat_mod = lambda t: (lambda i: i // (seq // t))
    ctx_mod = lambda i: bsz
    zero_state = jnp.zeros((bsz, GLA_HEADS, GLA_DK, GLA_DV), F32)

    for l in range(DEPTH):
        last = l == DEPTH - 1
        mod = mods[l]
        edges = [0]
        for s in (GLA_K, GLA_K, GLA_V, GLA_V, GLA_LR, GLA_LR, CONV_W, CONV_W, POOL_W, N_BRANCH * d):
            edges.append(edges[-1] + s)
        seg = lambda i, j=None: w_in[l, :, edges[i]:edges[i + 1 if j is None else j]].astype(BF16)
        w_lr = jnp.pad(seg(4, 6), ((0, 0), (0, LR_PAD - 2 * GLA_LR)))
        chunks = lambda a: [a[:, c:c + PRE_CHUNK] for c in range(0, a.shape[1], PRE_CHUNK)]
        w_plain = jnp.stack(chunks(seg(0)) + chunks(seg(1)) + chunks(seg(2)) + chunks(seg(8)))
        w_gated = jnp.stack(chunks(seg(3)))
        pre_w = (w_plain, w_gated, w_lr, seg(6), seg(7))
        conv_p = (w_dw[l], row(b_dw[l]), row(g_conv_ln[l]), row(b_conv_ln[l]))
        wd = jnp.zeros((LR_PAD, 2 * GLA_K), F32)
        wd = wd.at[:GLA_LR, :GLA_K].set(w_decay[l, 0]).at[GLA_LR:2 * GLA_LR, GLA_K:].set(w_decay[l, 1])
        wd = wd.astype(BF16)
        bd = b_decay[l].reshape(1, 2 * GLA_K)
        mix_w = (seg(9), b_gate[l].reshape(1, N_BRANCH * d), row(g_gla[l]), w_gla_o[l].astype(BF16),
                 w_conv_o[l].astype(BF16), w_pool_g[l].astype(BF16), row(s_pool[l]),
                 w_pool_o[l].astype(BF16), w_out[l].astype(BF16))
        w1, w2 = w_mlp1[l].astype(BF16), w_mlp2[l].astype(BF16)
        gpm, gqm = row(g_pre_mix[l]), row(g_post_mix[l])
        gpf, gqf = row(g_pre_mlp[l]), row(g_post_mlp[l])

        pa, pb, lr, cv = _pre(xf, mod, gpm, pre_w, conv_p, tm=tm_pre, row_len=GRID_W, mod_index=lat_mod(tm_pre))
        cpa, cpb, clr, ccv = _pre(cf, mod, gpm, pre_w, conv_p, tm=ctx_len, row_len=ctx_len, mod_index=ctx_mod)
        co_f, co_b, cs_f, cs_b = _gla(cpa, clr, wd, bd, zero_state, zero_state, batch=bsz, t=ctx_len)
        o_f, o_b, _, _ = _gla(pa, lr, wd, bd, cs_f, cs_b, batch=bsz, t=512)

        r4 = lambda a: a.reshape(a.shape[:-2] + (bsz, rows, GRID_W, a.shape[-1]))
        x1 = _mix_latent(r4(xf), mod, gpm, gqm, r4(o_f), r4(o_b), r4(pb), r4(cv), r4(pa), mix_w, rb=32, cb=16)
        xf = _mlp(x1.reshape(n_lat, d), mod, gpf, gqf, w1, w2, tm=tm_mlp, mod_index=lat_mod(tm_mlp))
        if not last:
            c1 = _mix_ctx(cf, mod, gpm, gqm, co_f, co_b, cpb, ccv, cpa, mix_w, batch=bsz)
            cf = _mlp(c1, mod, gpf, gqf, w1, w2, tm=min(tm_mlp, n_ctx), mod_index=ctx_mod)
    return xf.reshape(bsz, seq, d)
```

```python
import functools

import jax
import jax.numpy as jnp
from jax import lax
from jax.experimental import pallas as pl
from jax.experimental.pallas import tpu as pltpu

D_MODEL = 1024
DEPTH = 2
GRID_W = 64
GLA_HEADS = 4
GLA_DK = 128
GLA_DV = 256
GLA_K = GLA_HEADS * GLA_DK
GLA_V = GLA_HEADS * GLA_DV
GLA_LR = 16
GLA_TAU = 16.0
GLA_CHUNK = 64
GLA_BLOCK = 2 * GLA_CHUNK
CONV_W = 512
CONV_K = 31
CONV_HALO = 16
POOL_GROUPS = 4
POOL_W = 512
POOL_GC = 128
POOL_WINDOWS = (2, 4, 8, 16)
N_BRANCH = 3
D_FF = 4 * D_MODEL
EPS = 1e-6
LOG2E = 1.4426950408889634
LANES = 128
LR_PAD = LANES
PRE_CHUNK = 512
MLP_CHUNK = 2048
PA_Q, PA_K, PA_V, PA_POOL = 0, 1, 2, 4
MOD_ROWS = 16
VMEM_LIMIT = 56 * 1024 * 1024

F32 = jnp.float32
BF16 = jnp.bfloat16


def _dot(a, b):
    return jnp.dot(a, b, preferred_element_type=F32)


def _dot_nt(a, b):
    return lax.dot_general(a, b, (((1,), (1,)), ((), ())), preferred_element_type=F32)


def _dot_tn(a, b):
    return lax.dot_general(a, b, (((0,), (0,)), ((), ())), preferred_element_type=F32)


def _sigmoid(x):
    return jax.nn.sigmoid(x)


def _silu(x):
    return x * _sigmoid(x)


def _rms(x):
    return x * lax.rsqrt(jnp.mean(x * x, axis=-1, keepdims=True) + EPS)


def _cparams(sem):
    return pltpu.CompilerParams(dimension_semantics=sem, vmem_limit_bytes=VMEM_LIMIT)


def _full(shape):
    nd = len(shape)
    return pl.BlockSpec(shape, lambda *_: (0,) * nd)


def _ada_kernel(cc_ref, w_ref, b_ref, o_ref):
    a = _silu(cc_ref[...]).astype(BF16)
    o_ref[0] = _dot(a, w_ref[0].astype(BF16)) + b_ref[0]


def _ada(cc, w_ada, b_ada):
    tn = 1024
    n = 6 * D_MODEL
    return pl.pallas_call(
        _ada_kernel,
        grid=(DEPTH, n // tn),
        in_specs=[
            pl.BlockSpec((MOD_ROWS, D_MODEL), lambda l, j: (0, 0)),
            pl.BlockSpec((1, D_MODEL, tn), lambda l, j: (l, 0, j)),
            pl.BlockSpec((1, 1, tn), lambda l, j: (l, 0, j)),
        ],
        out_specs=pl.BlockSpec((1, MOD_ROWS, tn), lambda l, j: (l, 0, j)),
        out_shape=jax.ShapeDtypeStruct((DEPTH, MOD_ROWS, n), F32),
        compiler_params=_cparams(("parallel", "parallel")),
        name="ada",
    )(cc, w_ada, b_ada.reshape(DEPTH, 1, n))


def _pre_kernel(x_ref, mod_ref, g_ref, wa_ref, wb_ref, wlr_ref, wga_ref, wgb_ref,
                wdw_ref, bdw_ref, gln_ref, bln_ref,
                pa_ref, pb_ref, lr_ref, cv_ref, *, row_len):
    tm = x_ref.shape[0]
    h = _rms(x_ref[...]) * g_ref[...]
    h = h * (1.0 + mod_ref[0, 1:2, :]) + mod_ref[0, 0:1, :]
    hb = h.astype(BF16)
    u = _dot(hb, wga_ref[...]) * _sigmoid(_dot(hb, wgb_ref[...]))

    wp = row_len + 2 * CONV_HALO
    zeros = jnp.zeros((CONV_HALO, LANES), F32)
    row_blocks = []
    for r in range(tm // row_len):
        lane_blocks = []
        for c in range(CONV_W // LANES):
            cs = slice(c * LANES, (c + 1) * LANES)
            s = jnp.concatenate([zeros, u[r * row_len:(r + 1) * row_len, cs], zeros], axis=0)
            acc = None
            for j in range(CONV_K):
                d = j - CONV_K // 2
                sh = s if d == 0 else pltpu.roll(s, (wp - d) % wp, axis=0)
                term = sh[CONV_HALO:CONV_HALO + row_len, :] * wdw_ref[j:j + 1, cs]
                acc = term if acc is None else acc + term
            lane_blocks.append(acc + bdw_ref[:, cs])
        row_blocks.append(jnp.concatenate(lane_blocks, axis=1))
    y = jnp.concatenate(row_blocks, axis=0) if len(row_blocks) > 1 else row_blocks[0]
    yc = y - jnp.mean(y, axis=-1, keepdims=True)
    yn = yc * lax.rsqrt(jnp.mean(yc * yc, axis=-1, keepdims=True) + EPS) * gln_ref[...] + bln_ref[...]
    cv_ref[...] = _silu(yn).astype(BF16)

    for i in range(wa_ref.shape[0]):
        pa_ref[i] = _dot(hb, wa_ref[i]).astype(BF16)
    for i in range(wb_ref.shape[0]):
        pb_ref[i] = _silu(_dot(hb, wb_ref[i])).astype(BF16)
    lr_ref[...] = _dot(hb, wlr_ref[...]).astype(BF16)


def _pre(xf, mod, g, w, conv, *, tm, row_len, mod_index):
    n = xf.shape[0]
    wa, wb = w[0], w[1]
    tok = lambda c: pl.BlockSpec((tm, c), lambda i: (i, 0))
    stack = lambda k: pl.BlockSpec((k, tm, PRE_CHUNK), lambda i: (0, i, 0))
    return pl.pallas_call(
        functools.partial(_pre_kernel, row_len=row_len),
        grid=(n // tm,),
        in_specs=[tok(D_MODEL),
                  pl.BlockSpec((1, 6, D_MODEL), lambda i: (mod_index(i), 0, 0)),
                  _full((1, D_MODEL))]
                 + [_full(a.shape) for a in w]
                 + [_full(a.shape) for a in conv],
        out_specs=[stack(wa.shape[0]), stack(wb.shape[0]), tok(LR_PAD), tok(CONV_W)],
        out_shape=[jax.ShapeDtypeStruct((wa.shape[0], n, PRE_CHUNK), BF16),
                   jax.ShapeDtypeStruct((wb.shape[0], n, PRE_CHUNK), BF16),
                   jax.ShapeDtypeStruct((n, LR_PAD), BF16), jax.ShapeDtypeStruct((n, CONV_W), BF16)],
        compiler_params=_cparams(("parallel",)),
        name="pre",
    )(xf, mod, g, *w, *conv)


def _gla_kernel(qf_ref, kf_ref, vf_ref, lrf_ref, qb_ref, kb_ref, vb_ref, lrb_ref, wd_ref, bd_ref,
                s0f_ref, s0b_ref, of_ref, ob_ref, sf_ref, sb_ref):
    t = qf_ref.shape[0]
    nb = t // GLA_BLOCK

    @pl.when(pl.program_id(1) == 0)
    def _():
        sf_ref[...] = s0f_ref[...]
        sb_ref[...] = s0b_ref[...]

    brow = lax.broadcasted_iota(jnp.int32, (GLA_BLOCK, GLA_BLOCK), 0)
    bcol = lax.broadcasted_iota(jnp.int32, (GLA_BLOCK, GLA_BLOCK), 1)
    masks = (bcol <= brow, bcol >= brow)
    tris = tuple(jnp.where(m, 1.0, 0.0).astype(BF16) for m in masks)
    dirs = ((qf_ref, kf_ref, vf_ref, lrf_ref, of_ref, sf_ref),
            (qb_ref, kb_ref, vb_ref, lrb_ref, ob_ref, sb_ref))

    units = [(d, blk) for blk in range(nb) for d in (0, 1)]
    units = [(d, blk if d == 0 else nb - 1 - blk) for d, blk in units]
    rows = lambda blk: slice(blk * GLA_BLOCK, (blk + 1) * GLA_BLOCK)
    kcol = lambda h: slice(h * GLA_DK, (h + 1) * GLA_DK)
    vcol = lambda h: slice(h * GLA_DV, (h + 1) * GLA_DV)
    heads = range(GLA_HEADS)

    tri2 = [jnp.concatenate([tr, tr], axis=1) for tr in tris]
    per_chunk = PRE_CHUNK // GLA_DV

    def prep(us):
        zs = [_dot(dirs[d][3][rows(blk), :], wd_ref[:, d * GLA_K:(d + 1) * GLA_K])
              + bd_ref[:, d * GLA_K:(d + 1) * GLA_K] for d, blk in us]
        l2s = [(jnp.minimum(z, 0.0) * LOG2E - jnp.log2(1.0 + jnp.exp2(jnp.abs(z) * -LOG2E))) * (1.0 / GLA_TAU)
               for z in zs]
        his = [l2.astype(BF16) for l2 in l2s]
        los = [(l2 - hi.astype(F32)).astype(BF16) for l2, hi in zip(l2s, his)]
        cbs = [_dot(tri2[d], jnp.concatenate([hi, lo], axis=0)) for (d, _), hi, lo in zip(us, his, los)]
        ops = []
        for (d, blk), cb in zip(us, cbs):
            mid = GLA_CHUNK - 1 if d == 0 else GLA_CHUNK
            end = GLA_BLOCK - 1 if d == 0 else 0
            ref = cb[mid:mid + 1, :]
            last = cb[end:end + 1, :]
            e = jnp.exp2(cb - ref)
            qs = dirs[d][0][rows(blk), :].astype(F32) * (e * GLA_DK ** -0.5)
            kr = dirs[d][1][rows(blk), :].astype(F32) * (1.0 / e)
            q2 = qs * jnp.exp2(ref)
            ke = kr * jnp.exp2(last - ref)
            gamma = jnp.exp2(last)
            ops.append((qs.astype(BF16), q2.astype(BF16), kr.astype(BF16), ke.astype(BF16), gamma))
        return ops

    def score(us, ops):
        vs = [[dirs[d][2][h // per_chunk, rows(blk), (h % per_chunk) * GLA_DV:(h % per_chunk + 1) * GLA_DV]
               for h in heads] for d, blk in us]
        atts = [[jnp.where(masks[d], _dot_nt(op[0][:, kcol(h)], op[2][:, kcol(h)]), 0.0).astype(BF16)
                 for h in heads] for (d, _), op in zip(us, ops)]
        incs = [[_dot_tn(op[3][:, kcol(h)], v[h]) for h in heads] for op, v in zip(ops, vs)]
        return vs, atts, incs

    states = [[dirs[d][5][0, h] for h in heads] for d in (0, 1)]

    def carry(us, ops, incs):
        entering = []
        for (d, _), op, inc in zip(us, ops, incs):
            entering.append([st.astype(BF16) for st in states[d]])
            for h in heads:
                gcol = jnp.broadcast_to(op[4][:, kcol(h)], (GLA_DK, GLA_DK)).T
                states[d][h] = states[d][h] * jnp.concatenate([gcol] * (GLA_DV // GLA_DK), axis=1) + inc[h]
        return entering

    def emit(us, ops, atts, vs, entering):
        for (d, blk), op, att, v, st in zip(us, ops, atts, vs, entering):
            for h in heads:
                o = _dot(jnp.concatenate([att[h], op[1][:, kcol(h)]], axis=1),
                         jnp.concatenate([v[h], st[h]], axis=0))
                dirs[d][4][rows(blk), vcol(h)] = o.astype(BF16)

    ops = prep(units)
    vs, atts, incs = score(units, ops)
    entering = carry(units, ops, incs)
    emit(units, ops, atts, vs, entering)
    for d in (0, 1):
        for h in heads:
            dirs[d][5][0, h] = states[d][h]


def _gla(pa, lr, wd, bd, s0f, s0b, *, batch, t):
    n = pa.shape[1]
    nt = n // batch // t
    frow = lambda b, i: b * nt + i
    brow = lambda b, i: b * nt + nt - 1 - i
    tok = lambda c, row: pl.BlockSpec((t, c), lambda b, i: (row(b, i), 0))
    one = lambda chunk, row: pl.BlockSpec((None, t, PRE_CHUNK), lambda b, i: (chunk, row(b, i), 0))
    n_v = GLA_V // PRE_CHUNK
    val = lambda row: pl.BlockSpec((n_v, t, PRE_CHUNK), lambda b, i: (PA_V // n_v, row(b, i), 0))
    st = pl.BlockSpec((1, GLA_HEADS, GLA_DK, GLA_DV), lambda b, i: (b, 0, 0, 0))
    st_shape = jax.ShapeDtypeStruct((batch, GLA_HEADS, GLA_DK, GLA_DV), F32)
    return pl.pallas_call(
        _gla_kernel,
        grid=(batch, nt),
        in_specs=[one(PA_Q, frow), one(PA_K, frow), val(frow), tok(LR_PAD, frow),
                  one(PA_Q, brow), one(PA_K, brow), val(brow), tok(LR_PAD, brow),
                  _full(wd.shape), _full(bd.shape), st, st],
        out_specs=[tok(GLA_V, frow), tok(GLA_V, brow), st, st],
        out_shape=[jax.ShapeDtypeStruct((n, GLA_V), BF16), jax.ShapeDtypeStruct((n, GLA_V), BF16),
                   st_shape, st_shape],
        compiler_params=_cparams(("parallel", "arbitrary")),
        name="gla",
    )(pa, pa, pa, lr, pa, pa, pa, lr, wd, bd, s0f, s0b)


def _window_counts(n, left, right, shape, axis):
    t = lax.broadcasted_iota(jnp.int32, shape, axis)
    lo = jnp.clip(t - left, 0, n)
    hi = jnp.clip(t + right + 1, 0, n)
    return (hi - lo).astype(F32)


def _mix_kernel(x_ref, mod_ref, gpre_ref, gpost_ref, of_ref, ob_ref, sg_ref, cv_ref, up_ref,
                wgt_ref, bgt_ref, ggla_ref, wgo_ref, wco_ref, wpg_ref, sp_ref, wpo_ref, wout_ref,
                o_ref, pd_scr, *, lead):
    tm = pd_scr.shape[0] if not lead else x_ref.shape[0] * x_ref.shape[1]

    @pl.when(pl.program_id(2) == 0)
    def _():
        up = up_ref[...].astype(F32)
        n = up.shape[0]
        for g, w in enumerate(POOL_WINDOWS):
            left, right = w // 2, w - 1 - w // 2
            ug = up[..., g * POOL_GC:(g + 1) * POOL_GC]
            total = ug
            for d in range(-left, right + 1):
                if d == 0:
                    continue
                if lead:
                    pad = jnp.zeros((abs(d),) + ug.shape[1:], F32)
                    sh = (jnp.concatenate([ug[d:], pad], axis=0) if d > 0
                          else jnp.concatenate([pad, ug[:n + d]], axis=0))
                else:
                    tpos = lax.broadcasted_iota(jnp.int32, ug.shape, 0)
                    sh = jnp.where((tpos + d >= 0) & (tpos + d < n),
                                   pltpu.roll(ug, (n - d) % n, axis=0), 0.0)
                total = total + sh
            cnt = _window_counts(n, left, right, (n,) + (1,) * (ug.ndim - 1), 0)
            diff = (total / cnt - ug).reshape(-1, POOL_GC).astype(BF16)
            cs = slice(g * POOL_GC, (g + 1) * POOL_GC)
            pd_scr[:, cs] = (_dot(diff, wpg_ref[g]) * sp_ref[:, cs]).astype(BF16)

    parts = 2 if lead else 1
    tp = tm // parts
    lead_rows = x_ref.shape[0] // parts

    def part(ref, p, *idx):
        r = ref.at[idx] if idx else ref
        v = r[p * lead_rows:(p + 1) * lead_rows] if lead else r[p * tp:(p + 1) * tp, :]
        return v.reshape(tp, v.shape[-1])

    groups = range(parts)
    xs = [part(x_ref, p) for p in groups]
    hbs = [((_rms(x) * gpre_ref[...]) * (1.0 + mod_ref[0, 1:2, :]) + mod_ref[0, 0:1, :]).astype(BF16) for x in xs]

    def gla_in(p):
        o = part(of_ref, p).astype(F32) + part(ob_ref, p).astype(F32)
        on = jnp.concatenate([_rms(o[:, hh * GLA_DV:(hh + 1) * GLA_DV]) for hh in range(GLA_HEADS)], axis=-1)
        sg = jnp.concatenate([part(sg_ref, p, i) for i in range(sg_ref.shape[0])], axis=1)
        return (on * ggla_ref[...] * sg.astype(F32)).astype(BF16)

    a_ins = [gla_in(p) for p in groups]
    gate_pre = [[_dot(hb, wgt_ref[:, i * D_MODEL:(i + 1) * D_MODEL]) + bgt_ref[:, i * D_MODEL:(i + 1) * D_MODEL]
                 for i in range(N_BRANCH)] for hb in hbs]
    yas = [_dot(a, wgo_ref[...]) for a in a_ins]
    ybs = [_dot(part(cv_ref, p), wco_ref[...]) for p in groups]
    if lead:
        base = pl.multiple_of(pl.program_id(2) * tm, tm)
        pds = [pd_scr[pl.ds(base + p * tp, tp), :] for p in groups]
    else:
        pds = [pd_scr[...]]
    ycs = [_dot(pd, wpo_ref[...]) for pd in pds]
    mixed = [(_sigmoid(g[0]) * ya + _sigmoid(g[1]) * yb + _sigmoid(g[2]) * yc).astype(BF16)
             for g, ya, yb, yc in zip(gate_pre, yas, ybs, ycs)]
    ys = [_dot(m, wout_ref[...]) for m in mixed]
    for p, x, y in zip(groups, xs, ys):
        out = x + mod_ref[0, 2:3, :] * (_rms(y) * gpost_ref[...])
        if lead:
            o_ref[p * lead_rows:(p + 1) * lead_rows] = out.reshape((lead_rows,) + o_ref.shape[1:])
        else:
            o_ref[...] = out


def _mix_weight_specs(ws):
    return [_full(a.shape) for a in ws]


def _mix_latent(x4, mod, gpre, gpost, o_f, o_b, sg, cv, up, ws, *, rb, cb):
    bsz, rows, cols, _ = x4.shape
    blk = lambda c: pl.BlockSpec((None, rb, cb, c), lambda b, ci, ri: (b, ri, ci, 0))
    return pl.pallas_call(
        functools.partial(_mix_kernel, lead=True),
        grid=(bsz, cols // cb, rows // rb),
        in_specs=[blk(D_MODEL),
                  pl.BlockSpec((1, 6, D_MODEL), lambda b, ci, ri: (b, 0, 0)),
                  _full((1, D_MODEL)), _full((1, D_MODEL)),
                  blk(GLA_V), blk(GLA_V),
                  pl.BlockSpec((sg.shape[0], None, rb, cb, PRE_CHUNK), lambda b, ci, ri: (0, b, ri, ci, 0)),
                  blk(CONV_W),
                  pl.BlockSpec((None, None, rows, cb, POOL_W), lambda b, ci, ri: (PA_POOL, b, 0, ci, 0))]
                 + _mix_weight_specs(ws),
        out_specs=blk(D_MODEL),
        out_shape=jax.ShapeDtypeStruct(x4.shape, F32),
        scratch_shapes=[pltpu.VMEM((rows * cb, POOL_W), BF16)],
        compiler_params=_cparams(("parallel", "parallel", "arbitrary")),
        name="mix",
    )(x4, mod, gpre, gpost, o_f, o_b, sg, cv, up, *ws)


def _mix_ctx(xf, mod, gpre, gpost, o_f, o_b, sg, cv, up, ws, *, batch):
    n = xf.shape[0]
    tm = n // batch
    blk = lambda c: pl.BlockSpec((tm, c), lambda b, ci, ri: (b, 0))
    return pl.pallas_call(
        functools.partial(_mix_kernel, lead=False),
        grid=(batch, 1, 1),
        in_specs=[blk(D_MODEL),
                  pl.BlockSpec((1, 6, D_MODEL), lambda b, ci, ri: (batch, 0, 0)),
                  _full((1, D_MODEL)), _full((1, D_MODEL)),
                  blk(GLA_V), blk(GLA_V),
                  pl.BlockSpec((sg.shape[0], tm, PRE_CHUNK), lambda b, ci, ri: (0, b, 0)),
                  blk(CONV_W),
                  pl.BlockSpec((None, tm, POOL_W), lambda b, ci, ri: (PA_POOL, b, 0))]
                 + _mix_weight_specs(ws),
        out_specs=blk(D_MODEL),
        out_shape=jax.ShapeDtypeStruct(xf.shape, F32),
        scratch_shapes=[pltpu.VMEM((tm, POOL_W), BF16)],
        compiler_params=_cparams(("parallel", "parallel", "arbitrary")),
        name="mix_ctx",
    )(xf, mod, gpre, gpost, o_f, o_b, sg, cv, up, *ws)


def _mlp_kernel(x_ref, mod_ref, gpre_ref, gpost_ref, w1_ref, w2_ref, o_ref):
    tm = x_ref.shape[0]
    parts = 2 if tm >= 1024 else 1
    tp = tm // parts
    rows = lambda p: slice(p * tp, (p + 1) * tp)
    xs = [x_ref[rows(p), :] for p in range(parts)]
    hbs = [((_rms(x) * gpre_ref[...]) * (1.0 + mod_ref[0, 4:5, :]) + mod_ref[0, 3:4, :]).astype(BF16) for x in xs]
    ys = [None] * parts
    for c in range(D_FF // MLP_CHUNK):
        cs = slice(c * MLP_CHUNK, (c + 1) * MLP_CHUNK)
        acts = [jnp.maximum(_dot(hb, w1_ref[:, cs]), 0.0) for hb in hbs]
        for p, a in enumerate(acts):
            t = _dot((a * a).astype(BF16), w2_ref[cs, :])
            ys[p] = t if ys[p] is None else ys[p] + t
    for p, (x, y) in enumerate(zip(xs, ys)):
        o_ref[rows(p), :] = x + mod_ref[0, 5:6, :] * (_rms(y) * gpost_ref[...])


def _mlp(xf, mod, gpre, gpost, w1, w2, *, tm, mod_index):
    n = xf.shape[0]
    tok = pl.BlockSpec((tm, D_MODEL), lambda i: (i, 0))
    return pl.pallas_call(
        _mlp_kernel,
        grid=(n // tm,),
        in_specs=[tok, pl.BlockSpec((1, 6, D_MODEL), lambda i: (mod_index(i), 0, 0)),
                  _full((1, D_MODEL)), _full((1, D_MODEL)), _full(w1.shape), _full(w2.shape)],
        out_specs=tok,
        out_shape=jax.ShapeDtypeStruct(xf.shape, F32),
        compiler_params=_cparams(("parallel",)),
        name="mlp",
    )(xf, mod, gpre, gpost, w1, w2)


def kernel(x, c, ctx, c_ctx, w_ada, b_ada, g_pre_mix, g_post_mix, g_pre_mlp, g_post_mlp, w_in, w_decay,
           b_decay, g_gla, w_gla_o, w_dw, b_dw, g_conv_ln, b_conv_ln, w_conv_o, w_pool_g, s_pool,
           w_pool_o, b_gate, w_out, w_mlp1, w_mlp2):
    bsz, seq, d = x.shape
    ctx_len = ctx.shape[1]
    rows = seq // GRID_W
    n_lat, n_ctx = bsz * seq, bsz * ctx_len
    tm_pre, tm_mlp = 1024, 1024
    row = lambda a: a.reshape(1, -1)

    cc = jnp.zeros((MOD_ROWS, d), F32).at[:bsz].set(c).at[bsz].set(c_ctx)
    mods = _ada(cc, w_ada, b_ada).reshape(DEPTH, MOD_ROWS, 6, d)

    xf = x.reshape(n_lat, d)
    cf = ctx.reshape(n_ctx, d)
    lat_mod = lambda t: (lambda i: i // (seq // t))
    ctx_mod = lambda i: bsz
    zero_state = jnp.zeros((bsz, GLA_HEADS, GLA_DK, GLA_DV), F32)

    for l in range(DEPTH):
        last = l == DEPTH - 1
        mod = mods[l]
        edges = [0]
        for s in (GLA_K, GLA_K, GLA_V, GLA_V, GLA_LR, GLA_LR, CONV_W, CONV_W, POOL_W, N_BRANCH * d):
            edges.append(edges[-1] + s)
        seg = lambda i, j=None: w_in[l, :, edges[i]:edges[i + 1 if j is None else j]].astype(BF16)
        w_lr = jnp.pad(seg(4, 6), ((0, 0), (0, LR_PAD - 2 * GLA_LR)))
        chunks = lambda a: [a[:, c:c + PRE_CHUNK] for c in range(0, a.shape[1], PRE_CHUNK)]
        w_plain = jnp.stack(chunks(seg(0)) + chunks(seg(1)) + chunks(seg(2)) + chunks(seg(8)))
        w_gated = jnp.stack(chunks(seg(3)))
        pre_w = (w_plain, w_gated, w_lr, seg(6), seg(7))
        conv_p = (w_dw[l], row(b_dw[l]), row(g_conv_ln[l]), row(b_conv_ln[l]))
        wd = jnp.zeros((LR_PAD, 2 * GLA_K), F32)
        wd = wd.at[:GLA_LR, :GLA_K].set(w_decay[l, 0]).at[GLA_LR:2 * GLA_LR, GLA_K:].set(w_decay[l, 1])
        wd = wd.astype(BF16)
        bd = b_decay[l].reshape(1, 2 * GLA_K)
        mix_w = (seg(9), b_gate[l].reshape(1, N_BRANCH * d), row(g_gla[l]), w_gla_o[l].astype(BF16),
                 w_conv_o[l].astype(BF16), w_pool_g[l].astype(BF16), row(s_pool[l]),
                 w_pool_o[l].astype(BF16), w_out[l].astype(BF16))
        w1, w2 = w_mlp1[l].astype(BF16), w_mlp2[l].astype(BF16)
        gpm, gqm = row(g_pre_mix[l]), row(g_post_mix[l])
        gpf, gqf = row(g_pre_mlp[l]), row(g_post_mlp[l])

        pa, pb, lr, cv = _pre(xf, mod, gpm, pre_w, conv_p, tm=tm_pre, row_len=GRID_W, mod_index=lat_mod(tm_pre))
        cpa, cpb, clr, ccv = _pre(cf, mod, gpm, pre_w, conv_p, tm=ctx_len, row_len=ctx_len, mod_index=ctx_mod)
        co_f, co_b, cs_f, cs_b = _gla(cpa, clr, wd, bd, zero_state, zero_state, batch=bsz, t=ctx_len)
        o_f, o_b, _, _ = _gla(pa, lr, wd, bd, cs_f, cs_b, batch=bsz, t=1024)

        r4 = lambda a: a.reshape(a.shape[:-2] + (bsz, rows, GRID_W, a.shape[-1]))
        x1 = _mix_latent(r4(xf), mod, gpm, gqm, r4(o_f), r4(o_b), r4(pb), r4(cv), r4(pa), mix_w, rb=32, cb=16)
        xf = _mlp(x1.reshape(n_lat, d), mod, gpf, gqf, w1, w2, tm=tm_mlp, mod_index=lat_mod(tm_mlp))
        if not last:
            c1 = _mix_ctx(cf, mod, gpm, gqm, co_f, co_b, cpb, ccv, cpa, mix_w, batch=bsz)
            cf = _mlp(c1, mod, gpf, gqf, w1, w2, tm=min(tm_mlp, n_ctx), mod_index=ctx_mod)
    return xf.reshape(bsz, seq, d)
```
